```python
import math
import jax, jax.numpy as jnp
from jax import lax
import numpy as np

D_MODEL = 1024
BATCH = 2
SEQ = 8192
DEPTH = 1
DEC_BATCH = 32
DEC_SEQ = 4
PAST_LEN = 16384
PAGE_SIZE = 128

SSD_EXPAND = 2
SSD_D_INNER = SSD_EXPAND * D_MODEL
SSD_HEAD_DIM = 64
SSD_N_HEADS = SSD_D_INNER // SSD_HEAD_DIM
SSD_N_GROUPS = 4
SSD_HEADS_PER_GROUP = SSD_N_HEADS // SSD_N_GROUPS
SSD_D_STATE = 128
SSD_CONV = 4
SSD_CONV_DIM = SSD_D_INNER + 2 * SSD_N_GROUPS * SSD_D_STATE
SSD_CHUNK = 128
FOX_N_HEADS = 16
FOX_HEAD_DIM = 64
FOX_WIDTH = FOX_N_HEADS * FOX_HEAD_DIM
Q_BLOCK = 128
MEM_LEN = 256
MEM_N_HEADS = 4
MEM_HEAD_DIM = D_MODEL // MEM_N_HEADS
N_EXPERT_GROUPS = 4
EXPERTS_PER_GROUP = 8
N_EXPERTS = N_EXPERT_GROUPS * EXPERTS_PER_GROUP
TOP_K_IN_GROUP = 2
EXPERT_D_FF = 512
RMS_EPS = 1e-6
IN_WIDTHS = (SSD_D_INNER, SSD_CONV_DIM, SSD_N_HEADS, FOX_WIDTH, FOX_WIDTH, FOX_WIDTH, FOX_N_HEADS, 2 * D_MODEL)
IN_TOTAL = sum(IN_WIDTHS)

kernel_name = 'hybrid_ssd_fox_hmoe_decode_step'


def _rms_norm(x, g):
    xf = x.astype(jnp.float32)
    y = xf * lax.rsqrt(jnp.mean(xf * xf, axis=-1, keepdims=True) + RMS_EPS)
    return (y * g.astype(jnp.float32)).astype(x.dtype)


def _gated_group_rms_norm(y, z, g):
    shp = y.shape
    h = (y.astype(jnp.float32) * jax.nn.silu(z.astype(jnp.float32))).reshape(shp[:-1] + (SSD_N_GROUPS, -1))
    h = h * lax.rsqrt(jnp.mean(h * h, axis=-1, keepdims=True) + RMS_EPS)
    return h.reshape(shp) * g.astype(jnp.float32)


def _split_in(proj):
    return jnp.split(proj, np.cumsum(IN_WIDTHS)[:-1].tolist(), axis=-1)


def _causal_dwconv(u, prev, w, b):
    l = u.shape[1]
    up = jnp.concatenate([prev, u], axis=1)
    acc = b + up[:, 0:l] * w[0]
    for j in range(1, SSD_CONV):
        acc = acc + up[:, j:j + l] * w[j]
    return jax.nn.silu(acc), up[:, l:]


def _ssd_chunked(xh, dt, a, bm, cm, h0):
    b, l, nh, p = xh.shape
    g, e, n = SSD_N_GROUPS, SSD_HEADS_PER_GROUP, SSD_D_STATE
    t = min(l, SSD_CHUNK)
    c = l // t
    x = (xh * dt[..., None]).reshape(b, c, t, g, e, p)
    da = (dt * a).reshape(b, c, t, g, e)
    bc = bm.reshape(b, c, t, g, n)
    cc = cm.reshape(b, c, t, g, n)
    a_cs = jnp.cumsum(da, axis=2)
    causal = jnp.tril(jnp.ones((t, t), dtype=bool))[None, None, :, :, None, None]
    seg = a_cs[:, :, :, None] - a_cs[:, :, None, :]
    decay_ts = jnp.exp(jnp.where(causal, seg, -jnp.inf))
    cb = jnp.einsum('bctgn,bcsgn->bctsg', cc, bc)
    y_diag = jnp.einsum('bctsge,bcsgep->bctgep', cb[..., None] * decay_ts, x)
    x_to_end = x * jnp.exp(a_cs[:, :, -1:] - a_cs)[..., None]
    chunk_states = jnp.einsum('bctgn,bctgep->bcgepn', bc, x_to_end)
    chunk_decay = jnp.exp(a_cs[:, :, -1])

    def step(h, inp):
        s_c, d_c = inp
        return h * d_c[..., None, None] + s_c, h

    h_last, h_prev = lax.scan(step, h0.reshape(b, g, e, p, n),
                              (jnp.moveaxis(chunk_states, 1, 0), jnp.moveaxis(chunk_decay, 1, 0)))
    h_prev = jnp.moveaxis(h_prev, 0, 1)
    y_off = jnp.einsum('bctgn,bcgepn->bctgep', cc, h_prev) * jnp.exp(a_cs)[..., None]
    y = (y_diag + y_off).reshape(b, l, nh, p)
    return y, h_last.reshape(b, nh, p, n)


def _ssd_branch(z, xbc_raw, dt_raw, conv_prev, ssm_prev, conv_w, conv_b, dt_bias, a_log, d_skip, ssd_norm):
    b, l, _ = z.shape
    f32 = jnp.float32
    xbc, conv_new = _causal_dwconv(xbc_raw, conv_prev.astype(xbc_raw.dtype), conv_w, conv_b)
    xs, bm, cm = jnp.split(xbc, [SSD_D_INNER, SSD_D_INNER + SSD_N_GROUPS * SSD_D_STATE], axis=-1)
    xh = xs.reshape(b, l, SSD_N_HEADS, SSD_HEAD_DIM).astype(f32)
    dt = jax.nn.softplus(dt_raw.astype(f32) + dt_bias.astype(f32))
    a = -jnp.exp(a_log.astype(f32))
    y, h_last = _ssd_chunked(xh, dt, a,
                             bm.reshape(b, l, SSD_N_GROUPS, SSD_D_STATE).astype(f32),
                             cm.reshape(b, l, SSD_N_GROUPS, SSD_D_STATE).astype(f32),
                             ssm_prev.astype(f32))
    y = y + d_skip.astype(f32)[:, None] * xh
    y = _gated_group_rms_norm(y.reshape(b, l, SSD_D_INNER), z, ssd_norm)
    return y.astype(z.dtype), conv_new.astype(conv_prev.dtype), h_last.astype(ssm_prev.dtype)


def _fox_prompt(q, k, v, logf):
    b, s_len, nh, dh = q.shape
    nb = s_len // Q_BLOCK
    f_cum = jnp.cumsum(logf, axis=1)
    f_keys = jnp.transpose(f_cum, (0, 2, 1))
    q_blocks = jnp.moveaxis(q.reshape(b, nb, Q_BLOCK, nh, dh), 1, 0)
    f_blocks = jnp.moveaxis(f_cum.reshape(b, nb, Q_BLOCK, nh), 1, 0)
    pos_k = jnp.arange(s_len)

    def block(args):
        qi, fi, i = args
        s = jnp.einsum('bqhd,bkhd->bhqk', qi, k).astype(jnp.float32) * dh ** -0.5
        s = s + jnp.transpose(fi, (0, 2, 1))[..., None] - f_keys[:, :, None, :]
        pos_q = i * Q_BLOCK + jnp.arange(Q_BLOCK)
        s = jnp.where(pos_k[None, :] <= pos_q[:, None], s, -jnp.inf)
        p = jax.nn.softmax(s, axis=-1).astype(v.dtype)
        return jnp.einsum('bhqk,bkhd->bqhd', p, v)

    out = lax.map(block, (q_blocks, f_blocks, jnp.arange(nb)))
    return jnp.moveaxis(out, 0, 1).reshape(b, s_len, nh, dh)


def _fox_sample(q, k_new, v_new, logf_new, cache_k, cache_v, cache_logf, page_table, layer):
    bd, nq, nh, dh = q.shape
    kp = cache_k[layer, page_table].reshape(bd, -1, nh, dh)
    vp = cache_v[layer, page_table].reshape(bd, -1, nh, dh)
    lfp = cache_logf[layer, page_table].reshape(bd, -1, nh).astype(jnp.float32)
    n_past = kp.shape[1]
    g_past = lax.cumsum(lfp, axis=1, reverse=True) - lfp
    f_new = jnp.transpose(jnp.cumsum(logf_new, axis=1), (0, 2, 1))
    s_p = jnp.einsum('bqhd,bkhd->bhqk', q, kp).astype(jnp.float32) * dh ** -0.5
    s_p = s_p + f_new[..., None] + jnp.transpose(g_past, (0, 2, 1))[:, :, None, :]
    s_n = jnp.einsum('bqhd,bkhd->bhqk', q, k_new).astype(jnp.float32) * dh ** -0.5
    s_n = s_n + f_new[..., None] - f_new[:, :, None, :]
    s_n = jnp.where(jnp.tril(jnp.ones((nq, nq), dtype=bool)), s_n, -jnp.inf)
    p = jax.nn.softmax(jnp.concatenate([s_p, s_n], axis=-1), axis=-1)
    out = (jnp.einsum('bhqk,bkhd->bqhd', p[..., :n_past].astype(vp.dtype), vp)
           + jnp.einsum('bhqk,bkhd->bqhd', p[..., n_past:].astype(v_new.dtype), v_new))
    return out.astype(q.dtype)


def _memory_attend(h, mem_k, mem_v, w_q):
    b, l, _ = h.shape
    q = (h @ w_q).reshape(b, l, MEM_N_HEADS, MEM_HEAD_DIM)
    s = jnp.einsum('blhd,bmhd->bhlm', q, mem_k).astype(jnp.float32) * MEM_HEAD_DIM ** -0.5
    p = jax.nn.softmax(s, axis=-1).astype(mem_v.dtype)
    return jnp.einsum('bhlm,bmhd->blhd', p, mem_v).reshape(b, l, D_MODEL).astype(h.dtype)


def _hier_moe(h, w_rg, b_rg, w_re, b_re, w_eg, w_eu, w_ed):
    shp = h.shape
    hf = h.reshape(-1, shp[-1])
    g_logits = (hf @ w_rg).astype(jnp.float32) + b_rg.astype(jnp.float32)
    g_prob = jax.nn.softmax(g_logits, axis=-1)
    g_idx = jnp.argmax(g_logits, axis=-1)
    g_w = jnp.take_along_axis(g_prob, g_idx[:, None], axis=-1)
    e_logits = ((hf @ w_re).astype(jnp.float32) + b_re.astype(jnp.float32)).reshape(-1, N_EXPERT_GROUPS, EXPERTS_PER_GROUP)
    e_in = jnp.take_along_axis(e_logits, g_idx[:, None, None], axis=1)[:, 0]
    top_v, top_i = lax.top_k(e_in, TOP_K_IN_GROUP)
    w_tok = jax.nn.softmax(top_v, axis=-1) * g_w
    eid = g_idx[:, None] * EXPERTS_PER_GROUP + top_i
    combine = jnp.sum(jax.nn.one_hot(eid, N_EXPERTS, dtype=jnp.float32) * w_tok[..., None], axis=1).astype(h.dtype)
    out = jnp.zeros_like(hf)
    for e in range(N_EXPERTS):
        hid = jax.nn.silu(hf @ w_eg[e]) * (hf @ w_eu[e])
        out = out + combine[:, e:e + 1] * (hid @ w_ed[e])
    return out.reshape(shp)


def _layer(x, fox_attend, mem_k, mem_v, conv_prev, ssm_prev,
           norm_mix, w_in, conv_w, conv_b, dt_bias, a_log, d_skip, ssd_norm, w_ssd_out,
           b_fox_f, w_fox_out, b_gate, w_mix_out, norm_mem, w_mem_q, w_mem_o,
           norm_ffn, w_rg, b_rg, w_re, b_re, w_eg, w_eu, w_ed):
    b, l, _ = x.shape
    h = _rms_norm(x, norm_mix)
    z, xbc, dt_raw, q, k, v, f_raw, gates = _split_in(h @ w_in)
    y_ssd, conv_new, ssm_new = _ssd_branch(z, xbc, dt_raw, conv_prev, ssm_prev,
                                           conv_w, conv_b, dt_bias, a_log, d_skip, ssd_norm)
    q = q.reshape(b, l, FOX_N_HEADS, FOX_HEAD_DIM)
    k = k.reshape(b, l, FOX_N_HEADS, FOX_HEAD_DIM)
    v = v.reshape(b, l, FOX_N_HEADS, FOX_HEAD_DIM)
    logf = jax.nn.log_sigmoid(f_raw.astype(jnp.float32) + b_fox_f.astype(jnp.float32))
    y_fox = fox_attend(q, k, v, logf).reshape(b, l, FOX_WIDTH)
    g_ssd, g_fox = jnp.split(jax.nn.sigmoid(gates + b_gate), 2, axis=-1)
    merged = g_ssd * (y_ssd @ w_ssd_out) + g_fox * (y_fox @ w_fox_out)
    x = x + merged @ w_mix_out
    x = x + _memory_attend(_rms_norm(x, norm_mem), mem_k, mem_v, w_mem_q) @ w_mem_o
    x = x + _hier_moe(_rms_norm(x, norm_ffn), w_rg, b_rg, w_re, b_re, w_eg, w_eu, w_ed)
    return x, k, v, logf, conv_new, ssm_new


def setup_inputs(seed: int = 0) -> dict:
    key = jax.random.key(seed)
    ks = iter(jax.random.split(key, 64))
    f32 = jnp.float32

    def nrm(shape, scale):
        return scale * jax.random.normal(next(ks), shape, f32)

    n_pages = PAST_LEN // PAGE_SIZE
    n_pool = (DEC_BATCH * n_pages * 5) // 4
    perm = jax.random.permutation(next(ks), n_pool)
    page_table = perm[:DEC_BATCH * n_pages].reshape(DEC_BATCH, n_pages).astype(jnp.int32)
    dt0 = jnp.exp(jax.random.uniform(next(ks), (DEPTH, SSD_N_HEADS), f32, math.log(1e-3), math.log(1e-1)))
    d = D_MODEL
    return {
        'x_prompt': nrm((BATCH, SEQ, d), 1.0),
        'x_sample': nrm((DEC_BATCH, DEC_SEQ, d), 1.0),
        'mem_prompt': nrm((BATCH, MEM_LEN, d), 1.0),
        'cache_fox_k': nrm((DEPTH, n_pool, PAGE_SIZE, FOX_N_HEADS, FOX_HEAD_DIM), 1.0),
        'cache_fox_v': nrm((DEPTH, n_pool, PAGE_SIZE, FOX_N_HEADS, FOX_HEAD_DIM), 1.0),
        'cache_fox_logf': jax.nn.log_sigmoid(4.0 + nrm((DEPTH, n_pool, PAGE_SIZE, FOX_N_HEADS), 1.0)),
        'page_table': page_table,
        'cache_mem_k': nrm((DEPTH, DEC_BATCH, MEM_LEN, MEM_N_HEADS, MEM_HEAD_DIM), 1.0),
        'cache_mem_v': nrm((DEPTH, DEC_BATCH, MEM_LEN, MEM_N_HEADS, MEM_HEAD_DIM), 1.0),
        'state_conv': nrm((DEPTH, DEC_BATCH, SSD_CONV - 1, SSD_CONV_DIM), 1.0),
        'state_ssm': nrm((DEPTH, DEC_BATCH, SSD_N_HEADS, SSD_HEAD_DIM, SSD_D_STATE), 0.5),
        'norm_mix': 1.0 + nrm((DEPTH, d), 0.02),
        'w_in': nrm((DEPTH, d, IN_TOTAL), d ** -0.5),
        'conv_w': nrm((DEPTH, SSD_CONV, SSD_CONV_DIM), SSD_CONV ** -0.5),
        'conv_b': nrm((DEPTH, SSD_CONV_DIM), 0.01),
        'dt_bias': dt0 + jnp.log(-jnp.expm1(-dt0)),
        'a_log': jnp.log(jax.random.uniform(next(ks), (DEPTH, SSD_N_HEADS), f32, 1.0, 16.0)),
        'd_skip': 1.0 + nrm((DEPTH, SSD_N_HEADS), 0.02),
        'ssd_norm': 1.0 + nrm((DEPTH, SSD_D_INNER), 0.02),
        'w_ssd_out': nrm((DEPTH, SSD_D_INNER, d), SSD_D_INNER ** -0.5),
        'b_fox_f': jnp.linspace(1.0, 7.0, FOX_N_HEADS, dtype=f32)[None, :] + nrm((DEPTH, FOX_N_HEADS), 0.1),
        'w_fox_out': nrm((DEPTH, FOX_WIDTH, d), FOX_WIDTH ** -0.5),
        'b_gate': nrm((DEPTH, 2 * d), 0.01),
        'w_mix_out': nrm((DEPTH, d, d), d ** -0.5),
        'norm_mem': 1.0 + nrm((DEPTH, d), 0.02),
        'norm_mem_kv': 1.0 + nrm((DEPTH, d), 0.02),
        'w_mem_q': nrm((DEPTH, d, d), d ** -0.5),
        'w_mem_k': nrm((DEPTH, d, d), d ** -0.5),
        'w_mem_v': nrm((DEPTH, d, d), d ** -0.5),
        'w_mem_o': nrm((DEPTH, d, d), d ** -0.5),
        'norm_ffn': 1.0 + nrm((DEPTH, d), 0.02),
        'w_router_group': nrm((DEPTH, d, N_EXPERT_GROUPS), d ** -0.5),
        'b_router_group': nrm((DEPTH, N_EXPERT_GROUPS), 0.01),
        'w_router_expert': nrm((DEPTH, d, N_EXPERTS), d ** -0.5),
        'b_router_expert': nrm((DEPTH, N_EXPERTS), 0.01),
        'w_exp_gate': nrm((DEPTH, N_EXPERTS, d, EXPERT_D_FF), d ** -0.5),
        'w_exp_up': nrm((DEPTH, N_EXPERTS, d, EXPERT_D_FF), d ** -0.5),
        'w_exp_down': nrm((DEPTH, N_EXPERTS, EXPERT_D_FF, d), EXPERT_D_FF ** -0.5),
        'norm_final': 1.0 + nrm((d,), 0.02),
    }


def reference(x_prompt, x_sample, mem_prompt, cache_fox_k, cache_fox_v, cache_fox_logf, page_table,
              cache_mem_k, cache_mem_v, state_conv, state_ssm,
              norm_mix, w_in, conv_w, conv_b, dt_bias, a_log, d_skip, ssd_norm, w_ssd_out,
              b_fox_f, w_fox_out, b_gate, w_mix_out, norm_mem, norm_mem_kv, w_mem_q, w_mem_k, w_mem_v, w_mem_o,
              norm_ffn, w_router_group, b_router_group, w_router_expert, b_router_expert,
              w_exp_gate, w_exp_up, w_exp_down, norm_final):
    bp = x_prompt.shape[0]
    xp, xs = x_prompt, x_sample
    pk_l, pv_l, plf_l, pmk_l, pmv_l, pconv_l, pssm_l = [], [], [], [], [], [], []
    sk_l, sv_l, slf_l, sconv_l, sssm_l = [], [], [], [], []
    for l in range(DEPTH):
        lw = (norm_mix[l], w_in[l], conv_w[l], conv_b[l], dt_bias[l], a_log[l], d_skip[l], ssd_norm[l], w_ssd_out[l],
              b_fox_f[l], w_fox_out[l], b_gate[l], w_mix_out[l], norm_mem[l], w_mem_q[l], w_mem_o[l],
              norm_ffn[l], w_router_group[l], b_router_group[l], w_router_expert[l], b_router_expert[l],
              w_exp_gate[l], w_exp_up[l], w_exp_down[l])
        mem_h = _rms_norm(mem_prompt, norm_mem_kv[l])
        mk = (mem_h @ w_mem_k[l]).reshape(bp, MEM_LEN, MEM_N_HEADS, MEM_HEAD_DIM)
        mv = (mem_h @ w_mem_v[l]).reshape(bp, MEM_LEN, MEM_N_HEADS, MEM_HEAD_DIM)
        conv0 = jnp.zeros((bp, SSD_CONV - 1, SSD_CONV_DIM), xp.dtype)
        ssm0 = jnp.zeros((bp, SSD_N_HEADS, SSD_HEAD_DIM, SSD_D_STATE), jnp.float32)
        xp, pk, pv, plf, pconv, pssm = _layer(xp, _fox_prompt, mk, mv, conv0, ssm0, *lw)
        fox_s = lambda q, k, v, lf, _l=l: _fox_sample(q, k, v, lf, cache_fox_k, cache_fox_v, cache_fox_logf, page_table, _l)
        xs, sk, sv, slf, sconv, sssm = _layer(xs, fox_s, cache_mem_k[l], cache_mem_v[l], state_conv[l], state_ssm[l], *lw)
        pk_l.append(pk); pv_l.append(pv); plf_l.append(plf); pmk_l.append(mk); pmv_l.append(mv)
        pconv_l.append(pconv); pssm_l.append(pssm)
        sk_l.append(sk); sv_l.append(sv); slf_l.append(slf); sconv_l.append(sconv); sssm_l.append(sssm)
    y_prompt = _rms_norm(xp, norm_final)
    y_sample = _rms_norm(xs, norm_final)
    new_fox_k_prompt = jnp.stack(pk_l)
    new_fox_v_prompt = jnp.stack(pv_l)
    new_fox_logf_prompt = jnp.stack(plf_l)
    new_mem_k_prompt = jnp.stack(pmk_l)
    new_mem_v_prompt = jnp.stack(pmv_l)
    new_conv_prompt = jnp.stack(pconv_l)
    new_ssm_prompt = jnp.stack(pssm_l)
    new_fox_k_sample = jnp.stack(sk_l)
    new_fox_v_sample = jnp.stack(sv_l)
    new_fox_logf_sample = jnp.stack(slf_l)
    new_conv_sample = jnp.stack(sconv_l)
    new_ssm_sample = jnp.stack(sssm_l)
    return (y_prompt, y_sample,
            new_fox_k_prompt, new_fox_v_prompt, new_fox_logf_prompt,
            new_mem_k_prompt, new_mem_v_prompt, new_conv_prompt, new_ssm_prompt,
            new_fox_k_sample, new_fox_v_sample, new_fox_logf_sample,
            new_conv_sample, new_ssm_sample)
```

```python
import functools

import numpy as np
import jax
import jax.numpy as jnp
from jax import lax
from jax.experimental import pallas as pl
from jax.experimental.pallas import tpu as pltpu

F32 = jnp.float32
BF16 = jnp.bfloat16
HIGHEST = lax.Precision.HIGHEST

RMS_EPS = 1e-6
SSD_HEAD_DIM = 64
SSD_N_GROUPS = 4
SSD_D_STATE = 128
SSD_CONV = 4
FOX_N_HEADS = 16
FOX_HEAD_DIM = 64
MEM_N_HEADS = 4
N_EXPERT_GROUPS = 4
EXPERTS_PER_GROUP = 8
N_EXPERTS = N_EXPERT_GROUPS * EXPERTS_PER_GROUP

LANES = 128
VMEM_LIMIT = 56 * 1024 * 1024


def _params(*sem):
    return pltpu.CompilerParams(dimension_semantics=sem, vmem_limit_bytes=VMEM_LIMIT)


def _dot(a, b, **kw):
    return jnp.dot(a, b, preferred_element_type=F32, **kw)


def _dot_nt(a, b, **kw):
    return lax.dot_general(a, b, (((1,), (1,)), ((), ())), preferred_element_type=F32, **kw)


def _dot_tn(a, b, **kw):
    return lax.dot_general(a, b, (((0,), (0,)), ((), ())), preferred_element_type=F32, **kw)


def _softplus(x):
    return jnp.maximum(x, 0.0) + jnp.log1p(jnp.exp(-jnp.abs(x)))


def _sigmoid(x):
    return 1.0 / (1.0 + jnp.exp(-x))


def _silu(x):
    return x * _sigmoid(x)


def _rms(x, g):
    return x * lax.rsqrt(jnp.mean(x * x, axis=-1, keepdims=True) + RMS_EPS) * g


def _iota2(shape, dim):
    return lax.broadcasted_iota(jnp.int32, shape, dim)


def _rmsnorm_kernel(x_ref, g_ref, o_ref):
    o_ref[...] = _rms(x_ref[...].astype(F32), g_ref[...]).astype(o_ref.dtype)


def _rmsnorm(x, g, tm):
    m, d = x.shape
    return pl.pallas_call(
        _rmsnorm_kernel,
        grid=(m // tm,),
        in_specs=[pl.BlockSpec((tm, d), lambda i: (i, 0)), pl.BlockSpec((1, d), lambda i: (0, 0))],
        out_specs=pl.BlockSpec((tm, d), lambda i: (i, 0)),
        out_shape=jax.ShapeDtypeStruct((m, d), BF16),
        compiler_params=_params("parallel"),
        name="rmsnorm",
    )(x, g.reshape(1, d))


def _mm_kernel(a_ref, w_ref, *o_refs, segs):
    j = pl.program_id(1)
    acc = _dot(a_ref[...], w_ref[...])
    for (s, e, scale), o_ref in zip(segs, o_refs):
        def _store(o_ref=o_ref, scale=scale):
            o_ref[...] = (acc * scale if scale != 1.0 else acc).astype(o_ref.dtype)
        if len(segs) == 1:
            _store()
        else:
            pl.when((j >= s) & (j < e))(_store)


def _matmul(a, w, outs, tm, tn):
    m, k = a.shape
    n = w.shape[1]
    segs, specs, shapes, c = [], [], [], 0
    for width, dtype, scale in outs:
        s, e = c // tn, (c + width) // tn
        segs.append((s, e, scale))
        specs.append(pl.BlockSpec((tm, tn), lambda i, j, s=s, e=e: (i, jnp.clip(j - s, 0, e - s - 1))))
        shapes.append(jax.ShapeDtypeStruct((m, width), dtype))
        c += width
    assert c == n and all(wd % tn == 0 for wd, _, _ in outs)
    res = pl.pallas_call(
        functools.partial(_mm_kernel, segs=tuple(segs)),
        grid=(m // tm, n // tn),
        in_specs=[pl.BlockSpec((tm, k), lambda i, j: (i, 0)), pl.BlockSpec((k, tn), lambda i, j: (0, j))],
        out_specs=specs,
        out_shape=shapes,
        compiler_params=_params("parallel", "arbitrary"),
        name="matmul",
    )(a, w)
    return res


def _ssd_kernel(xbc_ref, z_ref, small_ref, dtT_ref, conv0_ref, ssm0_ref,
                convw_ref, convb_ref, dtb_ref, dtbc_ref, alog_ref, alogc_ref, dskip_ref, gnorm_ref, e_ref,
                y_ref, convn_ref, ssm_ref, convbuf, ybuf, *, T, CPS, valid, n_heads, cast):
    step = pl.program_id(1)
    d_inner = n_heads * SSD_HEAD_DIM
    gw = d_inner // SSD_N_GROUPS
    hpg = n_heads // SSD_N_GROUPS
    n = SSD_D_STATE
    k1 = SSD_CONV - 1

    @pl.when(step == 0)
    def _():
        ssm_ref[...] = ssm0_ref[...]
        convbuf[8 - k1:8, :] = conv0_ref[0]

    row_t = _iota2((T, T), 0)
    col_t = _iota2((T, T), 1)
    causal = row_t >= col_t
    lmat = causal.astype(F32)
    umat = (row_t <= col_t).astype(F32)
    lane_lo = _iota2((T, LANES), 1) < SSD_HEAD_DIM
    row_lo = _iota2((LANES, 1), 0) < SSD_HEAD_DIM
    a_row = -jnp.exp(alog_ref[...])
    a_col = -jnp.exp(alogc_ref[...])
    emat = e_ref[...]

    def chunk(c, carry):
        r0 = pl.multiple_of(c * T, T)
        raw = xbc_ref[0, pl.ds(r0, T), :]
        convbuf[8:8 + T, :] = raw
        acc = convb_ref[...] + convbuf[8 - k1:8 - k1 + T, :] * convw_ref[0:1, :]
        for j in range(1, SSD_CONV):
            acc = acc + convbuf[8 - k1 + j:8 - k1 + j + T, :] * convw_ref[j:j + 1, :]
        if valid is None:
            new_conv = raw[T - k1:T, :]
        else:
            new_conv = raw[valid - k1:valid, :]
        convbuf[8 - k1:8, :] = raw[T - k1:T, :]
        convn_ref[0] = new_conv
        xbc = _silu(acc)
        xs = xbc[:, :d_inner]
        bm = xbc[:, d_inner:d_inner + SSD_N_GROUPS * n]
        cm = xbc[:, d_inner + SSD_N_GROUPS * n:]

        dt = _softplus(small_ref[0, pl.ds(r0, T), :][:, :n_heads] + dtb_ref[...])
        dtT = _softplus(dtT_ref[0, c] + dtbc_ref[...])
        if valid is not None:
            dt = jnp.where(_iota2((T, n_heads), 0) < valid, dt, 0.0)
            dtT = jnp.where(_iota2((n_heads, T), 1) < valid, dtT, 0.0)
        a_cs = _dot(lmat, dt * a_row, precision=HIGHEST)
        a_csT = _dot(dtT * a_col, umat, precision=HIGHEST)
        eacs = jnp.exp(a_cs)
        dend = jnp.exp(a_cs[T - 1:T, :] - a_cs)
        dt_x = _dot(dt, emat, precision=HIGHEST)
        dend_x = _dot(dend, emat, precision=HIGHEST)
        eacs_x = _dot(eacs, emat, precision=HIGHEST)
        xdt = xs * dt_x
        xte = xdt * dend_x

        for g in range(SSD_N_GROUPS):
            bg = cast(bm[:, g * n:(g + 1) * n])
            cg = cast(cm[:, g * n:(g + 1) * n])
            cb = _dot_nt(cg, bg)
            for jp in range(hpg // 2):
                h1 = g * hpg + 2 * jp
                lo = h1 * SSD_HEAD_DIM
                xdt_p = cast(xdt[:, lo:lo + LANES])
                yd = []
                for h in (h1, h1 + 1):
                    seg = a_cs[:, h:h + 1] - a_csT[h:h + 1, :]
                    dec = jnp.exp(jnp.where(causal, seg, -jnp.inf))
                    yd.append(_dot(cast(cb * dec), xdt_p))
                y_diag = jnp.where(lane_lo, yd[0], yd[1])
                st = ssm_ref[0, h1:h1 + 2].reshape(2 * SSD_HEAD_DIM, n)
                y_off = _dot_nt(cg, cast(st)) * eacs_x[:, lo:lo + LANES]
                cs = _dot_tn(cast(xte[:, lo:lo + LANES]), bg)
                dl = jnp.where(row_lo, eacs[T - 1:T, h1:h1 + 1], eacs[T - 1:T, h1 + 1:h1 + 2])
                ssm_ref[0, h1:h1 + 2] = (st * dl + cs).reshape(2, SSD_HEAD_DIM, n)
                ybuf[:, lo:lo + LANES] = (y_diag + y_off) + dskip_ref[:, lo:lo + LANES] * xs[:, lo:lo + LANES]

        zz = z_ref[0, pl.ds(r0, T), :]
        hg = ybuf[...] * _silu(zz)
        for g in range(SSD_N_GROUPS):
            hgg = hg[:, g * gw:(g + 1) * gw]
            y_ref[0, pl.ds(r0, T), g * gw:(g + 1) * gw] = _rms(hgg, gnorm_ref[:, g * gw:(g + 1) * gw]).astype(y_ref.dtype)
        return carry

    lax.fori_loop(0, CPS, chunk, 0)


def _ssd(xbc, z, small, conv0, ssm0, conv_w, conv_b, dt_bias, a_log, d_skip, ssd_norm, *, T, CPS, valid):
    b, l, cdim = xbc.shape
    d_inner = z.shape[-1]
    nh = d_inner // SSD_HEAD_DIM
    rows = T * CPS
    nsteps = l // rows
    assert l % rows == 0 and (valid is None or (nsteps == 1 and CPS == 1 and valid >= SSD_CONV - 1))
    dtT = jnp.swapaxes(small[:, :, :nh].reshape(b, l // T, T, nh), 2, 3)
    emat = jnp.asarray(np.repeat(np.eye(nh, dtype=np.float32), SSD_HEAD_DIM, axis=1))
    dskip_x = jnp.repeat(d_skip.astype(F32), SSD_HEAD_DIM).reshape(1, d_inner)
    aligned = T % 16 == 0
    cast = (lambda v: v.astype(BF16)) if aligned else (lambda v: v)
    full = lambda shape: pl.BlockSpec(shape, lambda i, s: (0,) * len(shape))
    y, convn, ssmn = pl.pallas_call(
        functools.partial(_ssd_kernel, T=T, CPS=CPS, valid=valid, n_heads=nh, cast=cast),
        grid=(b, nsteps),
        in_specs=[
            pl.BlockSpec((1, rows, cdim), lambda i, s: (i, s, 0)),
            pl.BlockSpec((1, rows, d_inner), lambda i, s: (i, s, 0)),
            pl.BlockSpec((1, rows, LANES), lambda i, s: (i, s, 0)),
            pl.BlockSpec((1, CPS, nh, T), lambda i, s: (i, s, 0, 0)),
            pl.BlockSpec((1, SSD_CONV - 1, cdim), lambda i, s: (i, 0, 0)),
            pl.BlockSpec((1, nh, SSD_HEAD_DIM, SSD_D_STATE), lambda i, s: (i, 0, 0, 0)),
            full((SSD_CONV, cdim)), full((1, cdim)), full((1, nh)), full((nh, 1)), full((1, nh)), full((nh, 1)),
            full((1, d_inner)), full((1, d_inner)), full((nh, d_inner)),
        ],
        out_specs=[
            pl.BlockSpec((1, rows, d_inner), lambda i, s: (i, s, 0)),
            pl.BlockSpec((1, SSD_CONV - 1, cdim), lambda i, s: (i, 0, 0)),
            pl.BlockSpec((1, nh, SSD_HEAD_DIM, SSD_D_STATE), lambda i, s: (i, 0, 0, 0)),
        ],
        out_shape=[
            jax.ShapeDtypeStruct((b, l, d_inner), BF16 if aligned else F32),
            jax.ShapeDtypeStruct((b, SSD_CONV - 1, cdim), F32),
            jax.ShapeDtypeStruct((b, nh, SSD_HEAD_DIM, SSD_D_STATE), F32),
        ],
        scratch_shapes=[pltpu.VMEM((8 + T, cdim), F32), pltpu.VMEM((T, d_inner), F32)],
        compiler_params=_params("parallel", "arbitrary"),
        name="ssd",
    )(xbc, z, small, dtT, conv0, ssm0, conv_w, conv_b.reshape(1, cdim), dt_bias.reshape(1, nh), dt_bias.reshape(nh, 1),
      a_log.reshape(1, nh), a_log.reshape(nh, 1), dskip_x, ssd_norm.reshape(1, d_inner), emat)
    return y, convn, ssmn


_AUG = 3


def _split3(f):
    hi = f.astype(BF16).astype(F32)
    r = f - hi
    mid = r.astype(BF16).astype(F32)
    lo = (r - mid).astype(BF16).astype(F32)
    return hi, mid, lo


def _fox_gate_kernel(small_ref, q_ref, k_ref, bf_ref, pq_ref, pk_ref, cq_ref, ck_ref,
                     logf_ref, qa_ref, ka_ref, carry, *, tr, f_off):
    nh = FOX_N_HEADS

    @pl.when(pl.program_id(1) == 0)
    def _():
        carry[...] = jnp.zeros_like(carry)

    logf = -_softplus(-(small_ref[0][:, f_off:f_off + nh] + bf_ref[...]))
    logf_ref[0] = logf
    lmat = (_iota2((tr, tr), 0) >= _iota2((tr, tr), 1)).astype(F32)
    fcum = _dot(lmat, logf, precision=HIGHEST) + carry[...]
    carry[...] = fcum[tr - 1:tr, :]
    pieces = _split3(fcum)
    xq = cq_ref[...]
    xk = ck_ref[...]
    for i, p in enumerate(pieces):
        xq = xq + _dot(p, pq_ref[i])
        xk = xk + _dot(p, pk_ref[i])
    lane = _iota2((tr, LANES), 1)
    for h in range(nh):
        sel = (lane < FOX_HEAD_DIM) if h % 2 == 0 else (lane >= FOX_HEAD_DIM)
        pair = slice((h // 2) * LANES, (h // 2 + 1) * LANES)
        blk = slice(h * LANES, (h + 1) * LANES)
        qa_ref[0, h] = jnp.where(sel, q_ref[0][:, pair].astype(F32), xq[:, blk]).astype(BF16)
        ka_ref[0, h] = jnp.where(sel, k_ref[0][:, pair], xk[:, blk]).astype(BF16)


def _aug_tables():
    nh = FOX_N_HEADS
    pq = np.zeros((_AUG, nh, nh * LANES), np.float32)
    pk = np.zeros((_AUG, nh, nh * LANES), np.float32)
    cq = np.zeros((1, nh * LANES), np.float32)
    ck = np.zeros((1, nh * LANES), np.float32)
    for h in range(nh):
        off = h * LANES + (FOX_HEAD_DIM if h % 2 == 0 else 0)
        for i in range(_AUG):
            pq[i, h, off + i] = 1.0
            ck[0, off + i] = 1.0
            cq[0, off + _AUG + i] = 1.0
            pk[i, h, off + _AUG + i] = -1.0
    return jnp.asarray(pq), jnp.asarray(pk), jnp.asarray(cq), jnp.asarray(ck)


def _fox_gate(small, q, k, b_fox_f, *, tr, f_off):
    b, l, _ = small.shape
    nh = FOX_N_HEADS
    pq, pk, cq, ck = _aug_tables()
    full = lambda shape: pl.BlockSpec(shape, lambda i, s: (0,) * len(shape))
    return pl.pallas_call(
        functools.partial(_fox_gate_kernel, tr=tr, f_off=f_off),
        grid=(b, l // tr),
        in_specs=[
            pl.BlockSpec((1, tr, LANES), lambda i, s: (i, s, 0)),
            pl.BlockSpec((1, tr, nh * FOX_HEAD_DIM), lambda i, s: (i, s, 0)),
            pl.BlockSpec((1, tr, nh * FOX_HEAD_DIM), lambda i, s: (i, s, 0)),
            full((1, nh)), full((_AUG, nh, nh * LANES)), full((_AUG, nh, nh * LANES)),
            full((1, nh * LANES)), full((1, nh * LANES)),
        ],
        out_specs=[
            pl.BlockSpec((1, tr, nh), lambda i, s: (i, s, 0)),
            pl.BlockSpec((1, nh, tr, LANES), lambda i, s: (i, 0, s, 0)),
            pl.BlockSpec((1, nh, tr, LANES), lambda i, s: (i, 0, s, 0)),
        ],
        out_shape=[
            jax.ShapeDtypeStruct((b, l, nh), F32),
            jax.ShapeDtypeStruct((b, nh, l, LANES), BF16),
            jax.ShapeDtypeStruct((b, nh, l, LANES), BF16),
        ],
        scratch_shapes=[pltpu.VMEM((1, nh), F32)],
        compiler_params=_params("parallel", "arbitrary"),
        name="fox_gate",
    )(small, q, k, b_fox_f.reshape(1, nh), pq, pk, cq, ck)


def _fox_prompt_kernel(qi_ref, ki_ref, q_ref, k_ref, v_ref, o_ref, m_scr, l_scr, acc_scr, *, tq):
    s_idx = pl.program_id(2)
    qi = qi_ref[s_idx]
    ki = ki_ref[s_idx]

    @pl.when(ki == 0)
    def _():
        m_scr[...] = jnp.full_like(m_scr, -jnp.inf)
        l_scr[...] = jnp.zeros_like(l_scr)
        acc_scr[...] = jnp.zeros_like(acc_scr)

    def body(masked):
        v = v_ref[0].astype(BF16)
        for hh in range(2):
            s = _dot_nt(q_ref[0, hh], k_ref[0, hh])
            if masked:
                s = jnp.where(_iota2(s.shape, 1) <= _iota2(s.shape, 0), s, -jnp.inf)
            m_prev = m_scr[hh]
            m_new = jnp.maximum(m_prev, jnp.max(s, axis=1, keepdims=True))
            alpha = jnp.exp(m_prev - m_new)
            p = jnp.exp(s - m_new)
            l_scr[hh] = alpha * l_scr[hh] + jnp.sum(p, axis=1, keepdims=True)
            acc_scr[hh] = alpha * acc_scr[hh] + _dot(p.astype(BF16), v)
            m_scr[hh] = m_new

    pl.when(ki < qi)(lambda: body(False))

    @pl.when(ki == qi)
    def _():
        body(True)
        lane_lo = _iota2((tq, LANES), 1) < FOX_HEAD_DIM
        o_ref[0] = jnp.where(lane_lo, acc_scr[0] / l_scr[0], acc_scr[1] / l_scr[1]).astype(o_ref.dtype)


def _fox_prompt(q_aug, k_aug, v, *, tq):
    b, nh, l, _ = q_aug.shape
    nq = l // tq
    qi = np.array([i for i in range(nq) for _ in range(i + 1)], np.int32)
    ki = np.array([j for i in range(nq) for j in range(i + 1)], np.int32)
    grid_spec = pltpu.PrefetchScalarGridSpec(
        num_scalar_prefetch=2,
        grid=(b, nh // 2, len(qi)),
        in_specs=[
            pl.BlockSpec((1, 2, tq, LANES), lambda i, p, s, qi, ki: (i, p, qi[s], 0)),
            pl.BlockSpec((1, 2, tq, LANES), lambda i, p, s, qi, ki: (i, p, ki[s], 0)),
            pl.BlockSpec((1, tq, LANES), lambda i, p, s, qi, ki: (i, ki[s], p)),
        ],
        out_specs=pl.BlockSpec((1, tq, LANES), lambda i, p, s, qi, ki: (i, qi[s], p)),
        scratch_shapes=[pltpu.VMEM((2, tq, 1), F32), pltpu.VMEM((2, tq, 1), F32), pltpu.VMEM((2, tq, LANES), F32)],
    )
    return pl.pallas_call(
        functools.partial(_fox_prompt_kernel, tq=tq),
        grid_spec=grid_spec,
        out_shape=jax.ShapeDtypeStruct((b, l, nh * FOX_HEAD_DIM), BF16),
        compiler_params=_params("parallel", "parallel", "arbitrary"),
        name="fox_prompt",
    )(jnp.asarray(qi), jnp.asarray(ki), q_aug, k_aug, v)


def _fox_sample_kernel(pt_ref, q_ref, kn_ref, vn_ref, small_ref, bf_ref, *rest, PG, page, nq, f_off):
    k_refs = rest[:PG]
    v_refs = rest[PG:2 * PG]
    lf_refs = rest[2 * PG:3 * PG]
    o_ref, logf_ref, qx_scr, m_scr, l_scr, acc_scr, carry_scr = rest[3 * PG:]
    nh, dh = FOX_N_HEADS, FOX_HEAD_DIM
    rows = nq * nh
    width = nh * dh
    g = pl.program_id(1)
    head_of_row = _iota2((rows, width), 0) % nh
    own = (_iota2((rows, width), 1) // dh) == head_of_row

    logf_new = -_softplus(-(small_ref[0][:, f_off:f_off + nh] + bf_ref[...]))
    lf8 = jnp.where(_iota2((8, nh), 0) < nq, logf_new, 0.0)
    tri8 = (_iota2((8, 8), 0) <= _iota2((8, 8), 1)).astype(F32)
    fnew = _dot_tn(tri8, lf8, precision=HIGHEST)
    fnewT = _dot_tn(lf8, tri8, precision=HIGHEST)
    frow = jnp.concatenate([jnp.broadcast_to(fnew[qq:qq + 1, :], (nh, nh)) for qq in range(nq)], axis=0)
    fcol = jnp.sum(jnp.where(_iota2((rows, nh), 1) == _iota2((rows, nh), 0) % nh, frow, 0.0), axis=1, keepdims=True)

    @pl.when(g == 0)
    def _():
        qrows = jnp.concatenate([jnp.broadcast_to(q_ref[0][qq:qq + 1, :], (nh, width)) for qq in range(nq)], axis=0)
        qx_scr[...] = jnp.where(own, qrows, 0.0)
        m_scr[...] = jnp.full_like(m_scr, -jnp.inf)
        l_scr[...] = jnp.zeros_like(l_scr)
        acc_scr[...] = jnp.zeros_like(acc_scr)
        carry_scr[...] = jnp.zeros_like(carry_scr)
        logf_ref[0] = logf_new

    qx = qx_scr[...]
    qxb = qx.astype(BF16)
    ustrict = (_iota2((page, 2 * page), 0) > _iota2((page, 2 * page), 1)).astype(F32)
    utot = jnp.where(_iota2((page, 2 * page), 1) >= page, 1.0, ustrict)

    def online(s, vals):
        m_prev = m_scr[...]
        m_new = jnp.maximum(m_prev, jnp.max(s, axis=1, keepdims=True))
        alpha = jnp.exp(m_prev - m_new)
        p = jnp.exp(s - m_new)
        l_scr[...] = alpha * l_scr[...] + jnp.sum(p, axis=1, keepdims=True)
        acc_scr[...] = alpha * acc_scr[...] + _dot(p.astype(vals.dtype), vals)
        m_scr[...] = m_new

    for i in range(PG):
        gsum = _dot_tn(lf_refs[i][0], utot, precision=HIGHEST)
        gp = carry_scr[...] + gsum[:, :page]
        carry_scr[...] = carry_scr[...] + gsum[:, page:]
        s = _dot_nt(qxb, k_refs[i][0].astype(BF16))
        s = s + fcol + jnp.concatenate([gp] * nq, axis=0)
        online(s, v_refs[i][0].astype(BF16))

    @pl.when(g == pl.num_programs(1) - 1)
    def _():
        s = _dot_nt(qx, kn_ref[0])
        s = s + fcol - jnp.concatenate([fnewT] * nq, axis=0)
        keep = _iota2((rows, 8), 1) <= _iota2((rows, 8), 0) // nh
        online(jnp.where(keep, s, -jnp.inf), vn_ref[0])
        out = jnp.where(own, acc_scr[...] / l_scr[...], 0.0)
        pick = (_iota2((8, rows), 1) // nh == _iota2((8, rows), 0)).astype(F32)
        o_ref[0] = _dot(pick, out, precision=HIGHEST).astype(o_ref.dtype)


def _fox_sample(q, k_new, v_new, small, b_fox_f, cache_k, cache_v, cache_logf, page_table, *, nq, PG, f_off):
    b, _, width = q.shape
    nh = FOX_N_HEADS
    n_pages = page_table.shape[1]
    page = cache_k.shape[1]
    assert n_pages % PG == 0
    rows = nq * nh

    def pmap(i):
        return lambda bb, g, pt: (pt[bb, n_pages - 1 - (g * PG + i)], 0, 0)

    fixed = lambda shape: pl.BlockSpec(shape, lambda bb, g, pt: (bb,) + (0,) * (len(shape) - 1))
    in_specs = [fixed((1, 8, width)), fixed((1, 8, width)), fixed((1, 8, width)), fixed((1, 8, LANES)),
                pl.BlockSpec((1, nh), lambda bb, g, pt: (0, 0))]
    in_specs += [pl.BlockSpec((1, page, width), pmap(i)) for i in range(PG)]
    in_specs += [pl.BlockSpec((1, page, width), pmap(i)) for i in range(PG)]
    in_specs += [pl.BlockSpec((1, page, nh), pmap(i)) for i in range(PG)]
    grid_spec = pltpu.PrefetchScalarGridSpec(
        num_scalar_prefetch=1,
        grid=(b, n_pages // PG),
        in_specs=in_specs,
        out_specs=[fixed((1, 8, width)), fixed((1, 8, nh))],
        scratch_shapes=[pltpu.VMEM((rows, width), F32), pltpu.VMEM((rows, 1), F32), pltpu.VMEM((rows, 1), F32),
                        pltpu.VMEM((rows, width), F32), pltpu.VMEM((nh, page), F32)],
    )
    return pl.pallas_call(
        functools.partial(_fox_sample_kernel, PG=PG, page=page, nq=nq, f_off=f_off),
        grid_spec=grid_spec,
        out_shape=[jax.ShapeDtypeStruct((b, 8, width), F32), jax.ShapeDtypeStruct((b, 8, nh), F32)],
        compiler_params=_params("parallel", "arbitrary"),
        name="fox_sample",
    )(page_table, q, k_new, v_new, small, b_fox_f.reshape(1, nh),
      *([cache_k] * PG), *([cache_v] * PG), *([cache_logf] * PG))


def _mix_kernel(x_ref, ys_ref, yf_ref, gt_ref, bg_ref, wso_ref, wfo_ref, wmx_ref, nm_ref, wq_ref, x1_ref, qm_ref, *, qscale):
    d = x_ref.shape[-1]
    gate = _sigmoid(gt_ref[...] + bg_ref[...])
    merged = gate[:, :d] * _dot(ys_ref[...], wso_ref[...]) + gate[:, d:] * _dot(yf_ref[...], wfo_ref[...])
    x1 = x_ref[...] + _dot(merged.astype(BF16), wmx_ref[...])
    x1_ref[...] = x1
    h = _rms(x1, nm_ref[...]).astype(BF16)
    qm_ref[...] = (_dot(h, wq_ref[...]) * qscale).astype(qm_ref.dtype)


def _mix(x, y_ssd, y_fox, gates, b_gate, w_so, w_fo, w_mx, norm_mem, w_q, *, tm, qscale, q_dtype):
    m, d = x.shape
    row = lambda w: pl.BlockSpec((tm, w), lambda i: (i, 0))
    full = lambda a: pl.BlockSpec(a.shape, lambda i: (0,) * a.ndim)
    bg, nm = b_gate.reshape(1, -1), norm_mem.reshape(1, -1)
    return pl.pallas_call(
        functools.partial(_mix_kernel, qscale=qscale),
        grid=(m // tm,),
        in_specs=[row(d), row(y_ssd.shape[1]), row(d), row(2 * d), full(bg), full(w_so), full(w_fo), full(w_mx), full(nm), full(w_q)],
        out_specs=[row(d), row(d)],
        out_shape=[jax.ShapeDtypeStruct((m, d), F32), jax.ShapeDtypeStruct((m, d), q_dtype)],
        compiler_params=_params("parallel"),
        name="mix",
    )(x, y_ssd, y_fox, gates, bg, w_so, w_fo, w_mx, nm, w_q)


def _mem_attn_kernel(q_ref, k_ref, v_ref, o_ref, *, cast):
    dh = q_ref.shape[-1] // MEM_N_HEADS
    for h in range(MEM_N_HEADS):
        cols = slice(h * dh, (h + 1) * dh)
        s = _dot_nt(cast(q_ref[0][:, cols]), cast(k_ref[0][:, cols]))
        p = jnp.exp(s - jnp.max(s, axis=1, keepdims=True))
        p = p / jnp.sum(p, axis=1, keepdims=True)
        o_ref[0, :, cols] = _dot(cast(p), cast(v_ref[0][:, cols])).astype(o_ref.dtype)


def _mem_attn(q, mem_k, mem_v, *, tl):
    b, l, d = q.shape
    m = mem_k.shape[1]
    cast = (lambda v: v.astype(BF16)) if q.dtype == BF16 else (lambda v: v)
    return pl.pallas_call(
        functools.partial(_mem_attn_kernel, cast=cast),
        grid=(b, l // tl),
        in_specs=[pl.BlockSpec((1, tl, d), lambda i, s: (i, s, 0)),
                  pl.BlockSpec((1, m, d), lambda i, s: (i, 0, 0)),
                  pl.BlockSpec((1, m, d), lambda i, s: (i, 0, 0))],
        out_specs=pl.BlockSpec((1, tl, d), lambda i, s: (i, s, 0)),
        out_shape=jax.ShapeDtypeStruct((b, l, d), q.dtype),
        compiler_params=_params("parallel", "parallel"),
        name="mem_attn",
    )(q, mem_k, mem_v)


def _route_kernel(x1_ref, att_ref, wo_ref, nf_ref, wr_ref, br_ref, x2_ref, h_ref, rt_ref):
    x2 = x1_ref[...] + _dot(att_ref[...], wo_ref[...])
    x2_ref[...] = x2
    h = _rms(x2, nf_ref[...])
    h_ref[...] = h.astype(h_ref.dtype)
    logits = _dot(h, wr_ref[...], precision=HIGHEST) + br_ref[...]
    lane = _iota2(logits.shape, 1).astype(F32)
    first = lambda mask: jnp.min(jnp.where(mask, lane, float(LANES)), axis=1, keepdims=True)
    gl = jnp.where(lane < N_EXPERT_GROUPS, logits, -jnp.inf)
    gmax = jnp.max(gl, axis=1, keepdims=True)
    g_idx = first(gl == gmax)
    g_w = 1.0 / jnp.sum(jnp.exp(gl - gmax), axis=1, keepdims=True)
    e_lo = N_EXPERT_GROUPS + g_idx * EXPERTS_PER_GROUP
    el = jnp.where((lane >= e_lo) & (lane < e_lo + EXPERTS_PER_GROUP), logits, -jnp.inf)
    v1 = jnp.max(el, axis=1, keepdims=True)
    i1 = first(el == v1)
    el2 = jnp.where(lane == i1, -jnp.inf, el)
    v2 = jnp.max(el2, axis=1, keepdims=True)
    i2 = first(el2 == v2)
    e21 = jnp.exp(v2 - v1)
    w1 = g_w / (1.0 + e21)
    w2 = g_w * e21 / (1.0 + e21)
    rt_ref[...] = jnp.where(lane == 0, i1 - N_EXPERT_GROUPS, jnp.where(lane == 1, i2 - N_EXPERT_GROUPS,
                            jnp.where(lane == 2, w1, jnp.where(lane == 3, w2, 0.0))))


def _route(x1, att, w_o, norm_ffn, w_router, b_router, *, tm):
    m, d = x1.shape
    row = lambda w: pl.BlockSpec((tm, w), lambda i: (i, 0))
    full = lambda a: pl.BlockSpec(a.shape, lambda i: (0,) * a.ndim)
    nf = norm_ffn.reshape(1, d)
    return pl.pallas_call(
        _route_kernel,
        grid=(m // tm,),
        in_specs=[row(d), row(d), full(w_o), full(nf), full(w_router), full(b_router)],
        out_specs=[row(d), row(d), row(LANES)],
        out_shape=[jax.ShapeDtypeStruct((m, d), F32), jax.ShapeDtypeStruct((m, d), BF16), jax.ShapeDtypeStruct((m, LANES), F32)],
        compiler_params=_params("parallel"),
        name="route",
    )(x1, att, w_o, nf, w_router, b_router)


def _expert_kernel(te_ref, nt_ref, x_ref, wg_ref, wu_ref, wd_ref, o_ref):
    @pl.when(pl.program_id(0) < nt_ref[0])
    def _():
        x = x_ref[...]
        hid = _silu(_dot(x, wg_ref[0].astype(BF16))) * _dot(x, wu_ref[0].astype(BF16))
        o_ref[...] = _dot(hid.astype(BF16), wd_ref[0].astype(BF16)).astype(o_ref.dtype)

    @pl.when(pl.program_id(0) >= nt_ref[0])
    def _():
        o_ref[...] = jnp.zeros_like(o_ref)


def _experts(x_sorted, tile_expert, n_tiles, w_gate, w_up, w_down, *, tm):
    p, d = x_sorted.shape
    ne, _, ff = w_gate.shape
    grid_spec = pltpu.PrefetchScalarGridSpec(
        num_scalar_prefetch=2,
        grid=(p // tm,),
        in_specs=[pl.BlockSpec((tm, d), lambda i, te, nt: (i, 0)),
                  pl.BlockSpec((1, d, ff), lambda i, te, nt: (te[i], 0, 0)),
                  pl.BlockSpec((1, d, ff), lambda i, te, nt: (te[i], 0, 0)),
                  pl.BlockSpec((1, ff, d), lambda i, te, nt: (te[i], 0, 0))],
        out_specs=pl.BlockSpec((tm, d), lambda i, te, nt: (i, 0)),
    )
    return pl.pallas_call(
        _expert_kernel,
        grid_spec=grid_spec,
        out_shape=jax.ShapeDtypeStruct((p, d), F32),
        compiler_params=_params("arbitrary"),
        name="experts",
    )(tile_expert, n_tiles, x_sorted, w_gate, w_up, w_down)


def _final_kernel(x2_ref, ya_ref, yb_ref, rt_ref, g_ref, o_ref):
    rt = rt_ref[...]
    moe = rt[:, 2:3] * ya_ref[...] + rt[:, 3:4] * yb_ref[...]
    o_ref[...] = _rms(x2_ref[...] + moe, g_ref[...])


def _final(x2, ya, yb, rt, norm_final, *, tm):
    m, d = x2.shape
    row = lambda w: pl.BlockSpec((tm, w), lambda i: (i, 0))
    return pl.pallas_call(
        _final_kernel,
        grid=(m // tm,),
        in_specs=[row(d), row(d), row(d), row(LANES), pl.BlockSpec((1, d), lambda i: (0, 0))],
        out_specs=row(d),
        out_shape=jax.ShapeDtypeStruct((m, d), F32),
        compiler_params=_params("parallel"),
        name="final",
    )(x2, ya, yb, rt, norm_final.reshape(1, d))


def _dispatch_plan(eid, tm):
    n = eid.shape[0]
    p = (n + N_EXPERTS * (tm - 1) + tm - 1) // tm * tm
    onehot = (eid[:, None] == jnp.arange(N_EXPERTS, dtype=jnp.int32)[None, :]).astype(jnp.int32)
    csum = jnp.cumsum(onehot, axis=0)
    counts = csum[-1]
    rank = jnp.take_along_axis(csum, eid[:, None], axis=1)[:, 0] - 1
    padded = (counts + tm - 1) // tm * tm
    pend = jnp.cumsum(padded)
    pstart = pend - padded
    start = jnp.cumsum(counts) - counts
    pos = pstart[eid] + rank
    order = jnp.argsort(eid, stable=True).astype(jnp.int32)
    n_tiles = (pend[-1] // tm).astype(jnp.int32)
    tile_first = jnp.arange(p // tm, dtype=jnp.int32) * tm
    tile_expert = jnp.minimum(jnp.searchsorted(pend, tile_first, side="right"), N_EXPERTS - 1).astype(jnp.int32)
    last_used = tile_expert[jnp.maximum(n_tiles - 1, 0)]
    tile_expert = jnp.where(tile_first < pend[-1], tile_expert, last_used)
    slot = jnp.arange(p, dtype=jnp.int32)
    se = tile_expert[slot // tm]
    within = slot - pstart[se]
    src = jnp.where((within < counts[se]) & (slot < pend[-1]), order[jnp.clip(start[se] + within, 0, n - 1)], -1)
    return src, pos, tile_expert, n_tiles.reshape(1)


def _bf(w):
    return w.astype(BF16)


def kernel(x_prompt, x_sample, mem_prompt, cache_fox_k, cache_fox_v, cache_fox_logf, page_table, cache_mem_k, cache_mem_v, state_conv, state_ssm, norm_mix, w_in, conv_w, conv_b, dt_bias, a_log, d_skip, ssd_norm, w_ssd_out, b_fox_f, w_fox_out, b_gate, w_mix_out, norm_mem, norm_mem_kv, w_mem_q, w_mem_k, w_mem_v, w_mem_o, norm_ffn, w_router_group, b_router_group, w_router_expert, b_router_expert, w_exp_gate, w_exp_up, w_exp_down, norm_final):
    depth = w_in.shape[0]
    assert depth == 1, "single-layer step"
    bp, sp, d = x_prompt.shape
    bs, ss, _ = x_sample.shape
    tp, ts = bp * sp, bs * ss
    d_inner = ssd_norm.shape[-1]
    cdim = conv_w.shape[-1]
    nh_ssd = dt_bias.shape[-1]
    fw = FOX_N_HEADS * FOX_HEAD_DIM
    mem_len = mem_prompt.shape[1]
    l = 0

    o_z, o_x, o_dt, o_q, o_k, o_v, o_f, o_g = np.cumsum([0, d_inner, cdim, nh_ssd, fw, fw, fw, FOX_N_HEADS]).tolist()
    wi = w_in[l]
    w_main = _bf(jnp.concatenate([wi[:, o_z:o_x], wi[:, o_x:o_dt], wi[:, o_q:o_k], wi[:, o_k:o_v], wi[:, o_v:o_f], wi[:, o_g:]], axis=1))
    n_small = nh_ssd + FOX_N_HEADS
    w_small = _bf(jnp.pad(jnp.concatenate([wi[:, o_dt:o_q], wi[:, o_f:o_g]], axis=1), ((0, 0), (0, LANES - n_small))))
    f_off = nh_ssd
    main_outs = [(d_inner, F32, 1.0), (cdim, F32, 1.0), (fw, BF16, FOX_HEAD_DIM ** -0.5), (fw, F32, 1.0), (fw, F32, 1.0), (2 * d, F32, 1.0)]
    w_so, w_fo, w_mx, w_q, w_o = _bf(w_ssd_out[l]), _bf(w_fox_out[l]), _bf(w_mix_out[l]), _bf(w_mem_q[l]), _bf(w_mem_o[l])
    n_r = N_EXPERT_GROUPS + N_EXPERTS
    w_router = jnp.pad(jnp.concatenate([w_router_group[l], w_router_expert[l]], axis=1), ((0, 0), (0, LANES - n_r)))
    b_router = jnp.pad(jnp.concatenate([b_router_group[l], b_router_expert[l]]), (0, LANES - n_r)).reshape(1, LANES)
    mem_scale = (d // MEM_N_HEADS) ** -0.5

    def in_proj(x2d, tm):
        h = _rmsnorm(x2d, norm_mix[l], tm)
        z, xbc, q, k, v, gates = _matmul(h, w_main, main_outs, tm, 512)
        (small,) = _matmul(h, w_small, [(LANES, F32, 1.0)], tm, LANES)
        return z, xbc, q, k, v, gates, small

    def post(x2d, y_ssd, y_fox, gates, mem_k, mem_v, nb, tm, tl):
        per = x2d.shape[0] // nb
        q_dtype = BF16 if per % 16 == 0 else F32
        x1, qm = _mix(x2d, y_ssd, y_fox, gates, b_gate[l], w_so, w_fo, w_mx, norm_mem[l], w_q, tm=tm, qscale=mem_scale, q_dtype=q_dtype)
        qm = qm.reshape(nb, per, d)
        if per % 8:
            qm = jnp.pad(qm, ((0, 0), (0, 8 - per % 8), (0, 0)))
        att = _mem_attn(qm, mem_k, mem_v, tl=tl)[:, :per].reshape(-1, d).astype(BF16)
        return _route(x1, att, w_o, norm_ffn[l], w_router, b_router, tm=tm)

    xp = x_prompt.reshape(tp, d)
    z, xbc, q, k, v, gates, small = in_proj(xp, 512)
    conv0 = jnp.zeros((bp, SSD_CONV - 1, cdim), F32)
    ssm0 = jnp.zeros((bp, nh_ssd, SSD_HEAD_DIM, SSD_D_STATE), F32)
    y_ssd, pconv, pssm = _ssd(xbc.reshape(bp, sp, cdim), z.reshape(bp, sp, d_inner), small.reshape(bp, sp, LANES), conv0, ssm0,
                              conv_w[l], conv_b[l], dt_bias[l], a_log[l], d_skip[l], ssd_norm[l], T=128, CPS=4, valid=None)
    plogf, q_aug, k_aug = _fox_gate(small.reshape(bp, sp, LANES), q.reshape(bp, sp, fw), k.reshape(bp, sp, fw), b_fox_f[l], tr=512, f_off=f_off)
    y_fox = _fox_prompt(q_aug, k_aug, v.reshape(bp, sp, fw), tq=512)
    mem_h = _rmsnorm(mem_prompt.reshape(bp * mem_len, d), norm_mem_kv[l], 256)
    (mk,) = _matmul(mem_h, _bf(w_mem_k[l]), [(d, F32, 1.0)], 256, 512)
    (mv,) = _matmul(mem_h, _bf(w_mem_v[l]), [(d, F32, 1.0)], 256, 512)
    x2p, hp, rtp = post(xp, y_ssd.reshape(tp, d_inner), y_fox.reshape(tp, fw), gates, mk.reshape(bp, mem_len, d), mv.reshape(bp, mem_len, d), bp, 256, 512)

    xs = x_sample.reshape(ts, d)
    zs, xbcs, qs, ks, vs, gates_s, small_s = in_proj(xs, ts)
    pad8 = lambda a: jnp.pad(a.reshape(bs, ss, -1), ((0, 0), (0, 8 - ss), (0, 0)))
    ys_ssd, sconv, sssm = _ssd(pad8(xbcs), pad8(zs), pad8(small_s), state_conv[l], state_ssm[l],
                               conv_w[l], conv_b[l], dt_bias[l], a_log[l], d_skip[l], ssd_norm[l], T=8, CPS=1, valid=ss)
    pool, page = cache_fox_k.shape[1], cache_fox_k.shape[2]
    ys_fox, slogf = _fox_sample(pad8(qs.astype(F32)), pad8(ks), pad8(vs), pad8(small_s), b_fox_f[l],
                                cache_fox_k[l].reshape(pool, page, fw), cache_fox_v[l].reshape(pool, page, fw), cache_fox_logf[l],
                                page_table, nq=ss, PG=8, f_off=f_off)
    x2s, hs, rts = post(xs, ys_ssd[:, :ss].reshape(ts, d_inner).astype(BF16), ys_fox[:, :ss].reshape(ts, fw).astype(BF16), gates_s,
                        cache_mem_k[l].reshape(bs, mem_len, d), cache_mem_v[l].reshape(bs, mem_len, d), bs, ts, 8)

    tm_e = 256
    h_all = jnp.concatenate([hp, hs], axis=0)
    rt_all = jnp.concatenate([rtp, rts], axis=0)
    eid = rt_all[:, :2].astype(jnp.int32).reshape(-1)
    src, pos, tile_expert, n_tiles = _dispatch_plan(eid, tm_e)
    x_sorted = jnp.where((src >= 0)[:, None], jnp.take(h_all, jnp.maximum(src, 0) // 2, axis=0), jnp.zeros((), BF16))
    y_sorted = _experts(x_sorted, tile_expert, n_tiles, w_exp_gate[l], w_exp_up[l], w_exp_down[l], tm=tm_e)
    y_pair = jnp.take(y_sorted, pos, axis=0).reshape(tp + ts, 2, d)
    y_prompt = _final(x2p, y_pair[:tp, 0], y_pair[:tp, 1], rtp, norm_final, tm=512).reshape(bp, sp, d)
    y_sample = _final(x2s, y_pair[tp:, 0], y_pair[tp:, 1], rts, norm_final, tm=ts).reshape(bs, ss, d)

    hd = (FOX_N_HEADS, FOX_HEAD_DIM)
    return (y_prompt, y_sample,
            k.reshape(1, bp, sp, *hd), v.reshape(1, bp, sp, *hd), plogf.reshape(1, bp, sp, FOX_N_HEADS),
            mk.reshape(1, bp, mem_len, MEM_N_HEADS, d // MEM_N_HEADS), mv.reshape(1, bp, mem_len, MEM_N_HEADS, d // MEM_N_HEADS),
            pconv[None], pssm[None],
            ks.reshape(1, bs, ss, *hd), vs.reshape(1, bs, ss, *hd), slogf[:, :ss][None],
            sconv[None], sssm[None])
```

```python
import functools

import numpy as np
import jax
import jax.numpy as jnp
from jax import lax
from jax.experimental import pallas as pl
from jax.experimental.pallas import tpu as pltpu

F32 = jnp.float32
BF16 = jnp.bfloat16
HIGHEST = lax.Precision.HIGHEST

RMS_EPS = 1e-6
LOG2E = 1.4426950408889634
SSD_HEAD_DIM = 64
SSD_N_GROUPS = 4
SSD_D_STATE = 128
SSD_CONV = 4
FOX_N_HEADS = 16
FOX_HEAD_DIM = 64
MEM_N_HEADS = 4
N_EXPERT_GROUPS = 4
EXPERTS_PER_GROUP = 8
N_EXPERTS = N_EXPERT_GROUPS * EXPERTS_PER_GROUP

LANES = 128
VMEM_LIMIT = 56 * 1024 * 1024


def _params(*sem):
    return pltpu.CompilerParams(dimension_semantics=sem, vmem_limit_bytes=VMEM_LIMIT)


def _dot(a, b, **kw):
    return jnp.dot(a, b, preferred_element_type=F32, **kw)


def _dot_nt(a, b, **kw):
    return lax.dot_general(a, b, (((1,), (1,)), ((), ())), preferred_element_type=F32, **kw)


def _dot_tn(a, b, **kw):
    return lax.dot_general(a, b, (((0,), (0,)), ((), ())), preferred_element_type=F32, **kw)


def _softplus(x):
    return jnp.maximum(x, 0.0) + jnp.log1p(jnp.exp(-jnp.abs(x)))


def _sigmoid(x):
    return 1.0 / (1.0 + jnp.exp(-x))


def _silu(x):
    return x * _sigmoid(x)


def _rms(x, g):
    return x * lax.rsqrt(jnp.mean(x * x, axis=-1, keepdims=True) + RMS_EPS) * g


def _split3(f):
    hi = f.astype(BF16).astype(F32)
    r = f - hi
    mid = r.astype(BF16).astype(F32)
    lo = (r - mid).astype(BF16).astype(F32)
    return hi, mid, lo


def _iota2(shape, dim):
    return lax.broadcasted_iota(jnp.int32, shape, dim)


def _rmsnorm_kernel(x_ref, g_ref, o_ref):
    o_ref[...] = _rms(x_ref[...].astype(F32), g_ref[...]).astype(o_ref.dtype)


def _rmsnorm(x, g, tm):
    m, d = x.shape
    return pl.pallas_call(
        _rmsnorm_kernel,
        grid=(m // tm,),
        in_specs=[pl.BlockSpec((tm, d), lambda i: (i, 0)), pl.BlockSpec((1, d), lambda i: (0, 0))],
        out_specs=pl.BlockSpec((tm, d), lambda i: (i, 0)),
        out_shape=jax.ShapeDtypeStruct((m, d), BF16),
        compiler_params=_params("parallel"),
        name="rmsnorm",
    )(x, g.reshape(1, d))


def _mm_kernel(a_ref, w_ref, o_ref, *, scale):
    acc = _dot(a_ref[...], w_ref[0])
    o_ref[...] = (acc * scale if scale != 1.0 else acc).astype(o_ref.dtype)


def _tile_cols(w, tn):
    k, n = w.shape
    return jnp.swapaxes(w.reshape(k, n // tn, tn), 0, 1)


def _matmul(a, w_tiles, out_dtype, *, tm, scale=1.0):
    m, k = a.shape
    nt, _, tn = w_tiles.shape
    return pl.pallas_call(
        functools.partial(_mm_kernel, scale=scale),
        grid=(m // tm, nt),
        in_specs=[pl.BlockSpec((tm, k), lambda i, j: (i, 0)), pl.BlockSpec((1, k, tn), lambda i, j: (j, 0, 0))],
        out_specs=pl.BlockSpec((tm, tn), lambda i, j: (i, j)),
        out_shape=jax.ShapeDtypeStruct((m, nt * tn), out_dtype),
        compiler_params=_params("parallel", "arbitrary"),
        name="matmul",
    )(a, w_tiles)


def _ssd_kernel(xbc_ref, z_ref, small_ref, dtT_ref, conv0_ref, ssm0_ref,
                convw_ref, convb_ref, dtb_ref, dtbc_ref, alog_ref, alogc_ref, dskip_ref, gnorm_ref, e_ref,
                y_ref, convn_ref, ssm_ref, convbuf, ybuf, *, T, CPS, valid, n_heads, cast):
    step = pl.program_id(1)
    d_inner = n_heads * SSD_HEAD_DIM
    gw = d_inner // SSD_N_GROUPS
    hpg = n_heads // SSD_N_GROUPS
    n = SSD_D_STATE
    k1 = SSD_CONV - 1

    @pl.when(step == 0)
    def _():
        ssm_ref[...] = ssm0_ref[...]
        convbuf[8 - k1:8, :] = conv0_ref[0]

    row_t = _iota2((T, T), 0)
    col_t = _iota2((T, T), 1)
    causal = row_t >= col_t
    lmat = causal.astype(F32)
    umat = (row_t <= col_t).astype(F32)
    lane_lo = _iota2((T, LANES), 1) < SSD_HEAD_DIM
    row_lo = _iota2((LANES, 1), 0) < SSD_HEAD_DIM
    a_row = -jnp.exp(alog_ref[...])
    a_col = -jnp.exp(alogc_ref[...])
    emat = e_ref[...]

    def chunk(c, carry):
        r0 = pl.multiple_of(c * T, T)
        raw = xbc_ref[0, pl.ds(r0, T), :]
        convbuf[8:8 + T, :] = raw
        acc = convb_ref[...] + convbuf[8 - k1:8 - k1 + T, :] * convw_ref[0:1, :]
        for j in range(1, SSD_CONV):
            acc = acc + convbuf[8 - k1 + j:8 - k1 + j + T, :] * convw_ref[j:j + 1, :]
        if valid is None:
            new_conv = raw[T - k1:T, :]
        else:
            new_conv = raw[valid - k1:valid, :]
        convbuf[8 - k1:8, :] = raw[T - k1:T, :]
        convn_ref[0] = new_conv
        xbc = _silu(acc)
        xs = xbc[:, :d_inner]
        bm = xbc[:, d_inner:d_inner + SSD_N_GROUPS * n]
        cm = xbc[:, d_inner + SSD_N_GROUPS * n:]

        dt = _softplus(small_ref[0, pl.ds(r0, T), :][:, :n_heads] + dtb_ref[...])
        dtT = _softplus(dtT_ref[0, c] + dtbc_ref[...])
        if valid is not None:
            dt = jnp.where(_iota2((T, n_heads), 0) < valid, dt, 0.0)
            dtT = jnp.where(_iota2((n_heads, T), 1) < valid, dtT, 0.0)
        a_cs = _dot(lmat, dt * a_row, precision=HIGHEST)
        a_csT = _dot(dtT * a_col, umat, precision=HIGHEST)
        eacs = jnp.exp(a_cs)
        dend = jnp.exp(a_cs[T - 1:T, :] - a_cs)
        expand = lambda x: _dot(jnp.concatenate(_split3(x), axis=1), emat)
        dt_x = expand(dt)
        dend_x = expand(dend)
        eacs_x = expand(eacs)
        xdt = xs * dt_x
        xte = xdt * dend_x

        for g in range(SSD_N_GROUPS):
            bg = cast(bm[:, g * n:(g + 1) * n])
            cg = cast(cm[:, g * n:(g + 1) * n])
            cb = _dot_nt(cg, bg)
            for jp in range(hpg // 2):
                h1 = g * hpg + 2 * jp
                lo = h1 * SSD_HEAD_DIM
                xdt_p = cast(xdt[:, lo:lo + LANES])
                yd = []
                for h in (h1, h1 + 1):
                    seg = a_cs[:, h:h + 1] - a_csT[h:h + 1, :]
                    dec = jnp.exp(jnp.where(causal, seg, -jnp.inf))
                    yd.append(_dot(cast(cb * dec), xdt_p))
                y_diag = jnp.where(lane_lo, yd[0], yd[1])
                st = ssm_ref[0, h1:h1 + 2].reshape(2 * SSD_HEAD_DIM, n)
                y_off = _dot_nt(cg, cast(st)) * eacs_x[:, lo:lo + LANES]
                cs = _dot_tn(cast(xte[:, lo:lo + LANES]), bg)
                dl = jnp.where(row_lo, eacs[T - 1:T, h1:h1 + 1], eacs[T - 1:T, h1 + 1:h1 + 2])
                ssm_ref[0, h1:h1 + 2] = (st * dl + cs).reshape(2, SSD_HEAD_DIM, n)
                ybuf[:, lo:lo + LANES] = (y_diag + y_off) + dskip_ref[:, lo:lo + LANES] * xs[:, lo:lo + LANES]

        zz = z_ref[0, pl.ds(r0, T), :]
        hg = ybuf[...] * _silu(zz)
        for g in range(SSD_N_GROUPS):
            hgg = hg[:, g * gw:(g + 1) * gw]
            y_ref[0, pl.ds(r0, T), g * gw:(g + 1) * gw] = _rms(hgg, gnorm_ref[:, g * gw:(g + 1) * gw]).astype(y_ref.dtype)
        return carry

    lax.fori_loop(0, CPS, chunk, 0)


def _ssd(xbc, z, small, conv0, ssm0, conv_w, conv_b, dt_bias, a_log, d_skip, ssd_norm, *, T, CPS, valid):
    b, l, cdim = xbc.shape
    d_inner = z.shape[-1]
    nh = d_inner // SSD_HEAD_DIM
    rows = T * CPS
    nsteps = l // rows
    assert l % rows == 0 and (valid is None or (nsteps == 1 and CPS == 1 and valid >= SSD_CONV - 1))
    dtT = jnp.swapaxes(small[:, :, :nh].reshape(b, l // T, T, nh), 2, 3)
    emat = jnp.asarray(np.tile(np.repeat(np.eye(nh, dtype=np.float32), SSD_HEAD_DIM, axis=1), (3, 1)))
    dskip_x = jnp.repeat(d_skip.astype(F32), SSD_HEAD_DIM).reshape(1, d_inner)
    aligned = T % 16 == 0
    cast = (lambda v: v.astype(BF16)) if aligned else (lambda v: v)
    full = lambda shape: pl.BlockSpec(shape, lambda i, s: (0,) * len(shape))
    y, convn, ssmn = pl.pallas_call(
        functools.partial(_ssd_kernel, T=T, CPS=CPS, valid=valid, n_heads=nh, cast=cast),
        grid=(b, nsteps),
        in_specs=[
            pl.BlockSpec((1, rows, cdim), lambda i, s: (i, s, 0)),
            pl.BlockSpec((1, rows, d_inner), lambda i, s: (i, s, 0)),
            pl.BlockSpec((1, rows, LANES), lambda i, s: (i, s, 0)),
            pl.BlockSpec((1, CPS, nh, T), lambda i, s: (i, s, 0, 0)),
            pl.BlockSpec((1, SSD_CONV - 1, cdim), lambda i, s: (i, 0, 0)),
            pl.BlockSpec((1, nh, SSD_HEAD_DIM, SSD_D_STATE), lambda i, s: (i, 0, 0, 0)),
            full((SSD_CONV, cdim)), full((1, cdim)), full((1, nh)), full((nh, 1)), full((1, nh)), full((nh, 1)),
            full((1, d_inner)), full((1, d_inner)), full((3 * nh, d_inner)),
        ],
        out_specs=[
            pl.BlockSpec((1, rows, d_inner), lambda i, s: (i, s, 0)),
            pl.BlockSpec((1, SSD_CONV - 1, cdim), lambda i, s: (i, 0, 0)),
            pl.BlockSpec((1, nh, SSD_HEAD_DIM, SSD_D_STATE), lambda i, s: (i, 0, 0, 0)),
        ],
        out_shape=[
            jax.ShapeDtypeStruct((b, l, d_inner), BF16 if aligned else F32),
            jax.ShapeDtypeStruct((b, SSD_CONV - 1, cdim), F32),
            jax.ShapeDtypeStruct((b, nh, SSD_HEAD_DIM, SSD_D_STATE), F32),
        ],
        scratch_shapes=[pltpu.VMEM((8 + T, cdim), F32), pltpu.VMEM((T, d_inner), F32)],
        compiler_params=_params("parallel", "arbitrary"),
        name="ssd",
    )(xbc, z, small, dtT, conv0, ssm0, conv_w, conv_b.reshape(1, cdim), dt_bias.reshape(1, nh), dt_bias.reshape(nh, 1),
      a_log.reshape(1, nh), a_log.reshape(nh, 1), dskip_x, ssd_norm.reshape(1, d_inner), emat)
    return y, convn, ssmn


_AUG = 3


def _fox_gate_kernel(small_ref, q_ref, k_ref, bf_ref, pq_ref, pk_ref, cq_ref, ck_ref,
                     logf_ref, qa_ref, ka_ref, carry, *, tr, f_off):
    nh = FOX_N_HEADS

    @pl.when(pl.program_id(1) == 0)
    def _():
        carry[...] = jnp.zeros_like(carry)

    logf = -_softplus(-(small_ref[0][:, f_off:f_off + nh] + bf_ref[...]))
    logf_ref[0] = logf
    lmat = (_iota2((tr, tr), 0) >= _iota2((tr, tr), 1)).astype(F32)
    fcum = _dot(lmat, logf, precision=HIGHEST) + carry[...]
    carry[...] = fcum[tr - 1:tr, :]
    pieces = _split3(fcum * LOG2E)
    xq = cq_ref[...]
    xk = ck_ref[...]
    for i, p in enumerate(pieces):
        xq = xq + _dot(p, pq_ref[i])
        xk = xk + _dot(p, pk_ref[i])
    lane = _iota2((tr, LANES), 1)
    for h in range(nh):
        sel = (lane < FOX_HEAD_DIM) if h % 2 == 0 else (lane >= FOX_HEAD_DIM)
        pair = slice((h // 2) * LANES, (h // 2 + 1) * LANES)
        blk = slice(h * LANES, (h + 1) * LANES)
        qa_ref[0, h] = jnp.where(sel, q_ref[0][:, pair].astype(F32), xq[:, blk]).astype(BF16)
        ka_ref[0, h] = jnp.where(sel, k_ref[0][:, pair], xk[:, blk]).astype(BF16)


def _aug_tables():
    nh = FOX_N_HEADS
    pq = np.zeros((_AUG, nh, nh * LANES), np.float32)
    pk = np.zeros((_AUG, nh, nh * LANES), np.float32)
    cq = np.zeros((1, nh * LANES), np.float32)
    ck = np.zeros((1, nh * LANES), np.float32)
    for h in range(nh):
        off = h * LANES + (FOX_HEAD_DIM if h % 2 == 0 else 0)
        for i in range(_AUG):
            pq[i, h, off + i] = 1.0
            ck[0, off + i] = 1.0
            cq[0, off + _AUG + i] = 1.0
            pk[i, h, off + _AUG + i] = -1.0
    return jnp.asarray(pq), jnp.asarray(pk), jnp.asarray(cq), jnp.asarray(ck)


def _fox_gate(small, q, k, b_fox_f, *, tr, f_off):
    b, l, _ = small.shape
    nh = FOX_N_HEADS
    pq, pk, cq, ck = _aug_tables()
    full = lambda shape: pl.BlockSpec(shape, lambda i, s: (0,) * len(shape))
    return pl.pallas_call(
        functools.partial(_fox_gate_kernel, tr=tr, f_off=f_off),
        grid=(b, l // tr),
        in_specs=[
            pl.BlockSpec((1, tr, LANES), lambda i, s: (i, s, 0)),
            pl.BlockSpec((1, tr, nh * FOX_HEAD_DIM), lambda i, s: (i, s, 0)),
            pl.BlockSpec((1, tr, nh * FOX_HEAD_DIM), lambda i, s: (i, s, 0)),
            full((1, nh)), full((_AUG, nh, nh * LANES)), full((_AUG, nh, nh * LANES)),
            full((1, nh * LANES)), full((1, nh * LANES)),
        ],
        out_specs=[
            pl.BlockSpec((1, tr, nh), lambda i, s: (i, s, 0)),
            pl.BlockSpec((1, nh, tr, LANES), lambda i, s: (i, 0, s, 0)),
            pl.BlockSpec((1, nh, tr, LANES), lambda i, s: (i, 0, s, 0)),
        ],
        out_shape=[
            jax.ShapeDtypeStruct((b, l, nh), F32),
            jax.ShapeDtypeStruct((b, nh, l, LANES), BF16),
            jax.ShapeDtypeStruct((b, nh, l, LANES), BF16),
        ],
        scratch_shapes=[pltpu.VMEM((1, nh), F32)],
        compiler_params=_params("parallel", "arbitrary"),
        name="fox_gate",
    )(small, q, k, b_fox_f.reshape(1, nh), pq, pk, cq, ck)


def _fox_prompt_kernel(qi_ref, ki_ref, q_ref, k_ref, v_ref, o_ref, m_scr, acc_scr, *, tq, hps):
    s_idx = pl.program_id(2)
    qi = qi_ref[s_idx]
    ki = ki_ref[s_idx]

    @pl.when(ki == 0)
    def _():
        m_scr[...] = jnp.full_like(m_scr, -jnp.inf)
        acc_scr[...] = jnp.zeros_like(acc_scr)

    def body(masked):
        reps = tq // LANES
        lane_lo = _iota2((tq, LANES), 1) < FOX_HEAD_DIM
        if masked:
            keep = _iota2((tq, tq), 1) <= _iota2((tq, tq), 0)
        for hh in range(hps):
            pair = slice((hh // 2) * LANES, (hh // 2 + 1) * LANES)
            own = lane_lo if hh % 2 == 0 else jnp.logical_not(lane_lo)
            v1 = jnp.where(own, v_ref[0, :, pair], 1.0).astype(BF16)
            s = _dot_nt(q_ref[0, hh], k_ref[0, hh])
            if masked:
                s = jnp.where(keep, s, -jnp.inf)
            m_prev = m_scr[hh]
            m_new = jnp.maximum(m_prev, jnp.max(s, axis=1, keepdims=True))
            p = jnp.exp2(s - jnp.tile(m_new, (1, reps)))
            acc_scr[hh] = jnp.exp2(m_prev - m_new) * acc_scr[hh] + _dot(p.astype(BF16), v1)
            m_scr[hh] = m_new

    pl.when(ki < qi)(lambda: body(False))

    @pl.when(ki == qi)
    def _():
        body(True)
        lane_lo = _iota2((tq, LANES), 1) < FOX_HEAD_DIM
        for pp in range(hps // 2):
            a0, a1 = acc_scr[2 * pp], acc_scr[2 * pp + 1]
            o0 = a0 / pltpu.roll(a0, FOX_HEAD_DIM, axis=1)
            o1 = a1 / pltpu.roll(a1, FOX_HEAD_DIM, axis=1)
            o_ref[0, :, pp * LANES:(pp + 1) * LANES] = jnp.where(lane_lo, o0, o1).astype(o_ref.dtype)


def _fox_prompt(q_aug, k_aug, v, *, tq, hps):
    b, nh, l, _ = q_aug.shape
    nq = l // tq
    vw = hps * FOX_HEAD_DIM
    qi = np.array([i for i in range(nq) for _ in range(i + 1)], np.int32)
    ki = np.array([j for i in range(nq) for j in range(i + 1)], np.int32)
    grid_spec = pltpu.PrefetchScalarGridSpec(
        num_scalar_prefetch=2,
        grid=(b, nh // hps, len(qi)),
        in_specs=[
            pl.BlockSpec((1, hps, tq, LANES), lambda i, p, s, qi, ki: (i, p, qi[s], 0)),
            pl.BlockSpec((1, hps, tq, LANES), lambda i, p, s, qi, ki: (i, p, ki[s], 0)),
            pl.BlockSpec((1, tq, vw), lambda i, p, s, qi, ki: (i, ki[s], p)),
        ],
        out_specs=pl.BlockSpec((1, tq, vw), lambda i, p, s, qi, ki: (i, qi[s], p)),
        scratch_shapes=[pltpu.VMEM((hps, tq, LANES), F32), pltpu.VMEM((hps, tq, LANES), F32)],
    )
    return pl.pallas_call(
        functools.partial(_fox_prompt_kernel, tq=tq, hps=hps),
        grid_spec=grid_spec,
        out_shape=jax.ShapeDtypeStruct((b, l, nh * FOX_HEAD_DIM), BF16),
        compiler_params=_params("parallel", "parallel", "arbitrary"),
        name="fox_prompt",
    )(jnp.asarray(qi), jnp.asarray(ki), q_aug, k_aug, v)


def _lane_scan(x, shifts, *, width, suffix):
    lane = _iota2(x.shape, 1)
    for sh in shifts:
        if suffix:
            x = x + jnp.where(lane < width - sh, pltpu.roll(x, width - sh, axis=1), 0.0)
        else:
            x = x + jnp.where(lane >= sh, pltpu.roll(x, sh, axis=1), 0.0)
    return x


def _gate_scan_kernel(lf_ref, later_ref, total_ref, *, nh, page):
    strides = [nh << i for i in range((page - 1).bit_length())]
    lf = lf_ref[...]
    later_ref[...] = _lane_scan(lf, strides, width=page * nh, suffix=True) - lf
    total = lf
    for sh in strides:
        total = total + pltpu.roll(total, sh, axis=1)
    total_ref[...] = total


def _gate_scan(lf_flat, *, nh, rows):
    pool, pw = lf_flat.shape
    spec = pl.BlockSpec((rows, pw), lambda i: (i, 0))
    return pl.pallas_call(
        functools.partial(_gate_scan_kernel, nh=nh, page=pw // nh),
        grid=(pool // rows,),
        in_specs=[spec], out_specs=[spec, spec],
        out_shape=[jax.ShapeDtypeStruct((pool, pw), F32)] * 2,
        compiler_params=_params("parallel"),
        name="gate_scan",
    )(lf_flat)


def _fox_sample_kernel(pt_ref, q_ref, kn_ref, vn_ref, fr_ref, bf_ref, *rest, PG, page, nq, n_pages):
    k_refs = rest[:PG]
    v_refs = rest[PG:2 * PG]
    later_refs = rest[2 * PG:3 * PG]
    total_refs = rest[3 * PG:4 * PG]
    o_ref, logf_ref, bias_scr, m_scr, l_scr, acc_scr, carry_scr = rest[4 * PG:]
    nh, dh = FOX_N_HEADS, FOX_HEAD_DIM
    rows = nq * nh
    pw = page * nh
    bb = pl.program_id(0)
    g = pl.program_id(1)
    rr = _iota2((rows, LANES), 0)
    cc = _iota2((rows, LANES), 1)

    def new_gates():
        lf_new = -_softplus(-(fr_ref[0] + bf_ref[...]))
        f_new = _lane_scan(lf_new, [nh << i for i in range((nq - 1).bit_length())], width=LANES, suffix=False)
        f_col = jnp.sum(jnp.where(rr == cc, jnp.broadcast_to(f_new, (rows, LANES)), 0.0), axis=1, keepdims=True)
        return lf_new, f_new, f_col

    @pl.when(g == 0)
    def _():
        lf_new, _, f_col = new_gates()
        same = (_iota2((rows, pw), 1) % nh) == (_iota2((rows, pw), 0) % nh)
        bias_scr[...] = jnp.where(same, jnp.broadcast_to(f_col, (rows, pw)), -jnp.inf)
        m_scr[...] = jnp.full_like(m_scr, -jnp.inf)
        l_scr[...] = jnp.zeros_like(l_scr)
        acc_scr[...] = jnp.zeros_like(acc_scr)
        carry_scr[...] = jnp.zeros_like(carry_scr)
        logf_ref[0] = lf_new

    q2 = q_ref[0].reshape(rows, dh)

    def online(s, vals):
        m_prev = m_scr[...]
        m_new = jnp.maximum(m_prev, jnp.max(s, axis=1, keepdims=True))
        alpha = jnp.exp(m_prev - m_new)
        p = jnp.exp(s - m_new)
        l_scr[...] = alpha * l_scr[...] + jnp.sum(p, axis=1, keepdims=True)
        acc_scr[...] = alpha * acc_scr[...] + _dot(p, vals)
        m_scr[...] = m_new

    scores = []
    carry = carry_scr[...]
    for i in range(PG):
        sub = pt_ref[bb, n_pages - 1 - (g * PG + i)] % 8
        gate = carry + later_refs[i][pl.ds(sub, 1), :]
        carry = carry + total_refs[i][pl.ds(sub, 1), :]
        scores.append(_dot_nt(q2, k_refs[i][0, 0].reshape(pw, dh)) + bias_scr[...] + gate)
    carry_scr[...] = carry
    m_prev = m_scr[...]
    m_new = m_prev
    for s in scores:
        m_new = jnp.maximum(m_new, jnp.max(s, axis=1, keepdims=True))
    alpha = jnp.exp(m_prev - m_new)
    l_new = alpha * l_scr[...]
    acc = alpha * acc_scr[...]
    for i, s in enumerate(scores):
        p = jnp.exp(s - m_new)
        l_new = l_new + jnp.sum(p, axis=1, keepdims=True)
        acc = acc + _dot(p, v_refs[i][0, 0].reshape(pw, dh))
    m_scr[...] = m_new
    l_scr[...] = l_new
    acc_scr[...] = acc

    @pl.when(g == pl.num_programs(1) - 1)
    def _():
        _, f_new, f_col = new_gates()
        ok = (cc % nh == rr % nh) & (cc // nh <= rr // nh)
        s = _dot_nt(q2, kn_ref[0].reshape(rows, dh))
        s = jnp.where(ok[:, :rows], s + f_col - f_new[:, :rows], -jnp.inf)
        online(s, vn_ref[0].reshape(rows, dh))
        o_ref[0] = (acc_scr[...] / l_scr[...]).reshape(nq, nh, dh).astype(o_ref.dtype)


def _fox_sample(q, k_new, v_new, f_raw, b_fox_f, cache_k, cache_v, later, total, page_table, *, PG):
    b, nq, nh, dh = q.shape
    n_pages = page_table.shape[1]
    page = cache_k.shape[2]
    pw = page * nh
    rows = nq * nh
    assert n_pages % PG == 0 and rows <= LANES and page & (page - 1) == 0

    def pmap(i, kv):
        def im(bb, g, pt):
            pg = pt[bb, n_pages - 1 - (g * PG + i)]
            return (0, pg, 0, 0, 0) if kv else (pg // 8, 0)
        return im

    fixed = lambda shape: pl.BlockSpec(shape, lambda bb, g, pt: (bb,) + (0,) * (len(shape) - 1))
    bft = jnp.pad(jnp.tile(b_fox_f, nq), (0, LANES - rows)).reshape(1, LANES)
    in_specs = [fixed((1, nq, nh, dh)), fixed((1, nq, nh, dh)), fixed((1, nq, nh, dh)), fixed((1, 1, LANES)),
                pl.BlockSpec((1, LANES), lambda bb, g, pt: (0, 0))]
    in_specs += [pl.BlockSpec((1, 1, page, nh, dh), pmap(i, True)) for i in range(PG)]
    in_specs += [pl.BlockSpec((1, 1, page, nh, dh), pmap(i, True)) for i in range(PG)]
    in_specs += [pl.BlockSpec((8, pw), pmap(i, False)) for i in range(PG)]
    in_specs += [pl.BlockSpec((8, pw), pmap(i, False)) for i in range(PG)]
    grid_spec = pltpu.PrefetchScalarGridSpec(
        num_scalar_prefetch=1,
        grid=(b, n_pages // PG),
        in_specs=in_specs,
        out_specs=[fixed((1, nq, nh, dh)), fixed((1, 1, LANES))],
        scratch_shapes=[pltpu.VMEM((rows, pw), F32), pltpu.VMEM((rows, 1), F32), pltpu.VMEM((rows, 1), F32),
                        pltpu.VMEM((rows, dh), F32), pltpu.VMEM((1, pw), F32)],
    )
    return pl.pallas_call(
        functools.partial(_fox_sample_kernel, PG=PG, page=page, nq=nq, n_pages=n_pages),
        grid_spec=grid_spec,
        out_shape=[jax.ShapeDtypeStruct((b, nq, nh, dh), F32), jax.ShapeDtypeStruct((b, 1, LANES), F32)],
        compiler_params=_params("parallel", "arbitrary"),
        name="fox_sample",
    )(page_table, q, k_new, v_new, f_raw, bft,
      *([cache_k] * PG), *([cache_v] * PG), *([later] * PG), *([total] * PG))


def _mix_kernel(x_ref, ys_ref, yf_ref, gt_ref, bg_ref, wso_ref, wfo_ref, wmx_ref, nm_ref, wq_ref, x1_ref, qm_ref, *, qscale):
    d = x_ref.shape[-1]
    gate = _sigmoid(gt_ref[...] + bg_ref[...])
    merged = gate[:, :d] * _dot(ys_ref[...], wso_ref[...]) + gate[:, d:] * _dot(yf_ref[...], wfo_ref[...])
    x1 = x_ref[...] + _dot(merged.astype(BF16), wmx_ref[...])
    x1_ref[...] = x1
    h = _rms(x1, nm_ref[...]).astype(BF16)
    qm_ref[...] = (_dot(h, wq_ref[...]) * qscale).astype(qm_ref.dtype)


def _mix(x, y_ssd, y_fox, gates, b_gate, w_so, w_fo, w_mx, norm_mem, w_q, *, tm, qscale, q_dtype):
    m, d = x.shape
    row = lambda w: pl.BlockSpec((tm, w), lambda i: (i, 0))
    full = lambda a: pl.BlockSpec(a.shape, lambda i: (0,) * a.ndim)
    bg, nm = b_gate.reshape(1, -1), norm_mem.reshape(1, -1)
    return pl.pallas_call(
        functools.partial(_mix_kernel, qscale=qscale),
        grid=(m // tm,),
        in_specs=[row(d), row(y_ssd.shape[1]), row(d), row(2 * d), full(bg), full(w_so), full(w_fo), full(w_mx), full(nm), full(w_q)],
        out_specs=[row(d), row(d)],
        out_shape=[jax.ShapeDtypeStruct((m, d), F32), jax.ShapeDtypeStruct((m, d), q_dtype)],
        compiler_params=_params("parallel"),
        name="mix",
    )(x, y_ssd, y_fox, gates, bg, w_so, w_fo, w_mx, nm, w_q)


def _mem_attn_kernel(q_ref, k_ref, v_ref, o_ref, *, cast):
    dh = q_ref.shape[-1] // MEM_N_HEADS
    for h in range(MEM_N_HEADS):
        cols = slice(h * dh, (h + 1) * dh)
        s = _dot_nt(cast(q_ref[0][:, cols]), cast(k_ref[0][:, cols]))
        p = jnp.exp(s - jnp.max(s, axis=1, keepdims=True))
        p = p / jnp.sum(p, axis=1, keepdims=True)
        o_ref[0, :, cols] = _dot(cast(p), cast(v_ref[0][:, cols])).astype(o_ref.dtype)


def _mem_attn(q, mem_k, mem_v, *, tl):
    b, l, d = q.shape
    m = mem_k.shape[1]
    cast = (lambda v: v.astype(BF16)) if q.dtype == BF16 else (lambda v: v)
    return pl.pallas_call(
        functools.partial(_mem_attn_kernel, cast=cast),
        grid=(b, l // tl),
        in_specs=[pl.BlockSpec((1, tl, d), lambda i, s: (i, s, 0)),
                  pl.BlockSpec((1, m, d), lambda i, s: (i, 0, 0)),
                  pl.BlockSpec((1, m, d), lambda i, s: (i, 0, 0))],
        out_specs=pl.BlockSpec((1, tl, d), lambda i, s: (i, s, 0)),
        out_shape=jax.ShapeDtypeStruct((b, l, d), q.dtype),
        compiler_params=_params("parallel", "parallel"),
        name="mem_attn",
    )(q, mem_k, mem_v)


def _route_kernel(x1_ref, att_ref, wo_ref, nf_ref, wr_ref, br_ref, x2_ref, h_ref, rt_ref):
    x2 = x1_ref[...] + _dot(att_ref[...], wo_ref[...])
    x2_ref[...] = x2
    h = _rms(x2, nf_ref[...])
    h_ref[...] = h.astype(h_ref.dtype)
    logits = _dot(h, wr_ref[...], precision=HIGHEST) + br_ref[...]
    lane = _iota2(logits.shape, 1).astype(F32)
    first = lambda mask: jnp.min(jnp.where(mask, lane, float(LANES)), axis=1, keepdims=True)
    gl = jnp.where(lane < N_EXPERT_GROUPS, logits, -jnp.inf)
    gmax = jnp.max(gl, axis=1, keepdims=True)
    g_idx = first(gl == gmax)
    g_w = 1.0 / jnp.sum(jnp.exp(gl - gmax), axis=1, keepdims=True)
    e_lo = N_EXPERT_GROUPS + g_idx * EXPERTS_PER_GROUP
    el = jnp.where((lane >= e_lo) & (lane < e_lo + EXPERTS_PER_GROUP), logits, -jnp.inf)
    v1 = jnp.max(el, axis=1, keepdims=True)
    i1 = first(el == v1)
    el2 = jnp.where(lane == i1, -jnp.inf, el)
    v2 = jnp.max(el2, axis=1, keepdims=True)
    i2 = first(el2 == v2)
    e21 = jnp.exp(v2 - v1)
    w1 = g_w / (1.0 + e21)
    w2 = g_w * e21 / (1.0 + e21)
    rt_ref[...] = jnp.where(lane == 0, i1 - N_EXPERT_GROUPS, jnp.where(lane == 1, i2 - N_EXPERT_GROUPS,
                            jnp.where(lane == 2, w1, jnp.where(lane == 3, w2, 0.0))))


def _route(x1, att, w_o, norm_ffn, w_router, b_router, *, tm):
    m, d = x1.shape
    row = lambda w: pl.BlockSpec((tm, w), lambda i: (i, 0))
    full = lambda a: pl.BlockSpec(a.shape, lambda i: (0,) * a.ndim)
    nf = norm_ffn.reshape(1, d)
    return pl.pallas_call(
        _route_kernel,
        grid=(m // tm,),
        in_specs=[row(d), row(d), full(w_o), full(nf), full(w_router), full(b_router)],
        out_specs=[row(d), row(d), row(LANES)],
        out_shape=[jax.ShapeDtypeStruct((m, d), F32), jax.ShapeDtypeStruct((m, d), BF16), jax.ShapeDtypeStruct((m, LANES), F32)],
        compiler_params=_params("parallel"),
        name="route",
    )(x1, att, w_o, nf, w_router, b_router)


def _expert_kernel(te_ref, nt_ref, x_ref, wg_ref, wu_ref, wd_ref, o_ref):
    @pl.when(pl.program_id(0) < nt_ref[0])
    def _():
        x = x_ref[...]
        hid = _silu(_dot(x, wg_ref[0].astype(BF16))) * _dot(x, wu_ref[0].astype(BF16))
        o_ref[...] = _dot(hid.astype(BF16), wd_ref[0].astype(BF16)).astype(o_ref.dtype)

    @pl.when(pl.program_id(0) >= nt_ref[0])
    def _():
        o_ref[...] = jnp.zeros_like(o_ref)


def _experts(x_sorted, tile_expert, n_tiles, w_gate, w_up, w_down, *, tm):
    p, d = x_sorted.shape
    ne, _, ff = w_gate.shape
    grid_spec = pltpu.PrefetchScalarGridSpec(
        num_scalar_prefetch=2,
        grid=(p // tm,),
        in_specs=[pl.BlockSpec((tm, d), lambda i, te, nt: (i, 0)),
                  pl.BlockSpec((1, d, ff), lambda i, te, nt: (te[i], 0, 0)),
                  pl.BlockSpec((1, d, ff), lambda i, te, nt: (te[i], 0, 0)),
                  pl.BlockSpec((1, ff, d), lambda i, te, nt: (te[i], 0, 0))],
        out_specs=pl.BlockSpec((tm, d), lambda i, te, nt: (i, 0)),
    )
    return pl.pallas_call(
        _expert_kernel,
        grid_spec=grid_spec,
        out_shape=jax.ShapeDtypeStruct((p, d), F32),
        compiler_params=_params("arbitrary"),
        name="experts",
    )(tile_expert, n_tiles, x_sorted, w_gate, w_up, w_down)


def _final_kernel(x2_ref, ya_ref, yb_ref, rt_ref, g_ref, o_ref):
    rt = rt_ref[...]
    moe = rt[:, 2:3] * ya_ref[...] + rt[:, 3:4] * yb_ref[...]
    o_ref[...] = _rms(x2_ref[...] + moe, g_ref[...])


def _final(x2, ya, yb, rt, norm_final, *, tm):
    m, d = x2.shape
    row = lambda w: pl.BlockSpec((tm, w), lambda i: (i, 0))
    return pl.pallas_call(
        _final_kernel,
        grid=(m // tm,),
        in_specs=[row(d), row(d), row(d), row(LANES), pl.BlockSpec((1, d), lambda i: (0, 0))],
        out_specs=row(d),
        out_shape=jax.ShapeDtypeStruct((m, d), F32),
        compiler_params=_params("parallel"),
        name="final",
    )(x2, ya, yb, rt, norm_final.reshape(1, d))


def _dispatch_plan(eid, tm):
    n = eid.shape[0]
    p = (n + N_EXPERTS * (tm - 1) + tm - 1) // tm * tm
    onehot = (eid[:, None] == jnp.arange(N_EXPERTS, dtype=jnp.int32)[None, :]).astype(jnp.int32)
    csum = jnp.cumsum(onehot, axis=0)
    counts = csum[-1]
    rank = jnp.sum(csum * onehot, axis=1) - 1
    padded = (counts + tm - 1) // tm * tm
    pend = jnp.cumsum(padded)
    pstart = pend - padded
    pos = jnp.sum(pstart[None, :] * onehot, axis=1) + rank
    src = jnp.full((p,), -1, jnp.int32).at[pos].set(jnp.arange(n, dtype=jnp.int32), unique_indices=True)
    n_tiles = (pend[-1] // tm).astype(jnp.int32)
    tile_first = jnp.arange(p // tm, dtype=jnp.int32) * tm
    tile_expert = jnp.sum((tile_first[:, None] >= pend[None, :]).astype(jnp.int32), axis=1)
    last_used = jnp.sum((jnp.maximum(pend[-1] - tm, 0) >= pend).astype(jnp.int32))
    tile_expert = jnp.where(tile_first < pend[-1], tile_expert, last_used).astype(jnp.int32)
    return src, pos, tile_expert, n_tiles.reshape(1)


def _bf(w):
    return w.astype(BF16)


def kernel(x_prompt, x_sample, mem_prompt, cache_fox_k, cache_fox_v, cache_fox_logf, page_table, cache_mem_k, cache_mem_v, state_conv, state_ssm, norm_mix, w_in, conv_w, conv_b, dt_bias, a_log, d_skip, ssd_norm, w_ssd_out, b_fox_f, w_fox_out, b_gate, w_mix_out, norm_mem, norm_mem_kv, w_mem_q, w_mem_k, w_mem_v, w_mem_o, norm_ffn, w_router_group, b_router_group, w_router_expert, b_router_expert, w_exp_gate, w_exp_up, w_exp_down, norm_final):
    depth = w_in.shape[0]
    assert depth == 1, "single-layer step"
    bp, sp, d = x_prompt.shape
    bs, ss, _ = x_sample.shape
    tp, ts = bp * sp, bs * ss
    d_inner = ssd_norm.shape[-1]
    cdim = conv_w.shape[-1]
    nh_ssd = dt_bias.shape[-1]
    nh, dh = FOX_N_HEADS, FOX_HEAD_DIM
    fw = nh * dh
    mem_len = mem_prompt.shape[1]
    l = 0

    o_z, o_x, o_dt, o_q, o_k, o_v, o_f, o_g = np.cumsum([0, d_inner, cdim, nh_ssd, fw, fw, fw, nh]).tolist()
    wi = w_in[l]
    tn = 1024
    seg = lambda a, b: _tile_cols(_bf(wi[:, a:b]), tn)
    w_z, w_x, w_qf, w_kf, w_vf, w_g = seg(o_z, o_x), seg(o_x, o_dt), seg(o_q, o_k), seg(o_k, o_v), seg(o_v, o_f), seg(o_g, o_g + 2 * d)
    n_small = nh_ssd + nh
    w_small = _bf(jnp.pad(jnp.concatenate([wi[:, o_dt:o_q], wi[:, o_f:o_g]], axis=1), ((0, 0), (0, LANES - n_small))))[None]
    f_off = nh_ssd
    w_so, w_fo, w_mx, w_q, w_o = _bf(w_ssd_out[l]), _bf(w_fox_out[l]), _bf(w_mix_out[l]), _bf(w_mem_q[l]), _bf(w_mem_o[l])
    n_r = N_EXPERT_GROUPS + N_EXPERTS
    w_router = jnp.pad(jnp.concatenate([w_router_group[l], w_router_expert[l]], axis=1), ((0, 0), (0, LANES - n_r)))
    b_router = jnp.pad(jnp.concatenate([b_router_group[l], b_router_expert[l]]), (0, LANES - n_r)).reshape(1, LANES)
    mem_scale = (d // MEM_N_HEADS) ** -0.5

    def in_proj(x2d, tm, q_dtype, q_scale):
        h = _rmsnorm(x2d, norm_mix[l], tm)
        mm = lambda w, dt=F32, sc=1.0: _matmul(h, w, dt, tm=tm, scale=sc)
        return mm(w_z), mm(w_x), mm(w_qf, q_dtype, q_scale), mm(w_kf), mm(w_vf), mm(w_g), mm(w_small)

    def post(x2d, y_ssd, y_fox, gates, mem_k, mem_v, nb, tm, tl):
        per = x2d.shape[0] // nb
        q_dtype = BF16 if per % 16 == 0 else F32
        x1, qm = _mix(x2d, y_ssd, y_fox, gates, b_gate[l], w_so, w_fo, w_mx, norm_mem[l], w_q, tm=tm, qscale=mem_scale, q_dtype=q_dtype)
        qm = qm.reshape(nb, per, d)
        if per % 8:
            qm = jnp.pad(qm, ((0, 0), (0, 8 - per % 8), (0, 0)))
        att = _mem_attn(qm, mem_k, mem_v, tl=tl)[:, :per].reshape(-1, d).astype(BF16)
        return _route(x1, att, w_o, norm_ffn[l], w_router, b_router, tm=tm)

    xp = x_prompt.reshape(tp, d)
    z, xbc, q, k, v, gates, small = in_proj(xp, 1024, BF16, dh ** -0.5 * LOG2E)
    conv0 = jnp.zeros((bp, SSD_CONV - 1, cdim), F32)
    ssm0 = jnp.zeros((bp, nh_ssd, SSD_HEAD_DIM, SSD_D_STATE), F32)
    y_ssd, pconv, pssm = _ssd(xbc.reshape(bp, sp, cdim), z.reshape(bp, sp, d_inner), small.reshape(bp, sp, LANES), conv0, ssm0,
                              conv_w[l], conv_b[l], dt_bias[l], a_log[l], d_skip[l], ssd_norm[l], T=128, CPS=4, valid=None)
    plogf, q_aug, k_aug = _fox_gate(small.reshape(bp, sp, LANES), q.reshape(bp, sp, fw), k.reshape(bp, sp, fw), b_fox_f[l], tr=512, f_off=f_off)
    y_fox = _fox_prompt(q_aug, k_aug, v.reshape(bp, sp, fw), tq=1024, hps=4)
    mem_h = _rmsnorm(mem_prompt.reshape(bp * mem_len, d), norm_mem_kv[l], 256)
    mk = _matmul(mem_h, _tile_cols(_bf(w_mem_k[l]), 512), F32, tm=256)
    mv = _matmul(mem_h, _tile_cols(_bf(w_mem_v[l]), 512), F32, tm=256)
    x2p, hp, rtp = post(xp, y_ssd.reshape(tp, d_inner), y_fox.reshape(tp, fw), gates, mk.reshape(bp, mem_len, d), mv.reshape(bp, mem_len, d), bp, 256, 512)

    xs = x_sample.reshape(ts, d)
    zs, xbcs, qs, ks, vs, gates_s, small_s = in_proj(xs, ts, F32, dh ** -0.5)
    pad8 = lambda a: jnp.pad(a.reshape(bs, ss, -1), ((0, 0), (0, 8 - ss), (0, 0)))
    ys_ssd, sconv, sssm = _ssd(pad8(xbcs), pad8(zs), pad8(small_s), state_conv[l], state_ssm[l],
                               conv_w[l], conv_b[l], dt_bias[l], a_log[l], d_skip[l], ssd_norm[l], T=8, CPS=1, valid=ss)
    pool, page = cache_fox_k.shape[1], cache_fox_k.shape[2]
    later, total = _gate_scan(cache_fox_logf[l].reshape(pool, page * nh), nh=nh, rows=256)
    heads = lambda a: a.reshape(bs, ss, nh, dh)
    f_raw = jnp.pad(small_s[:, f_off:f_off + nh].reshape(bs, 1, ss * nh), ((0, 0), (0, 0), (0, LANES - ss * nh)))
    ys_fox, slogf = _fox_sample(heads(qs), heads(ks), heads(vs), f_raw, b_fox_f[l], cache_fox_k, cache_fox_v, later, total, page_table, PG=8)
    x2s, hs, rts = post(xs, ys_ssd[:, :ss].reshape(ts, d_inner).astype(BF16), ys_fox.reshape(ts, fw).astype(BF16), gates_s,
                        cache_mem_k[l].reshape(bs, mem_len, d), cache_mem_v[l].reshape(bs, mem_len, d), bs, ts, 8)

    tm_e = 256
    h_all = jnp.concatenate([hp, hs], axis=0)
    rt_all = jnp.concatenate([rtp, rts], axis=0)
    eid = rt_all[:, :2].astype(jnp.int32).reshape(-1)
    src, pos, tile_expert, n_tiles = _dispatch_plan(eid, tm_e)
    x_sorted = jnp.take(h_all, jnp.maximum(src, 0) // 2, axis=0)
    y_sorted = _experts(x_sorted, tile_expert, n_tiles, w_exp_gate[l], w_exp_up[l], w_exp_down[l], tm=tm_e)
    pos2 = pos.reshape(tp + ts, 2)
    pick = lambda rows, slot: jnp.take(y_sorted, pos2[rows, slot], axis=0)
    y_prompt = _final(x2p, pick(slice(0, tp), 0), pick(slice(0, tp), 1), rtp, norm_final, tm=512).reshape(bp, sp, d)
    y_sample = _final(x2s, pick(slice(tp, tp + ts), 0), pick(slice(tp, tp + ts), 1), rts, norm_final, tm=ts).reshape(bs, ss, d)

    return (y_prompt, y_sample,
            k.reshape(1, bp, sp, nh, dh), v.reshape(1, bp, sp, nh, dh), plogf.reshape(1, bp, sp, nh),
            mk.reshape(1, bp, mem_len, MEM_N_HEADS, d // MEM_N_HEADS), mv.reshape(1, bp, mem_len, MEM_N_HEADS, d // MEM_N_HEADS),
            pconv[None], pssm[None],
            ks.reshape(1, bs, ss, nh, dh), vs.reshape(1, bs, ss, nh, dh), slogf[:, 0, :ss * nh].reshape(1, bs, ss, nh),
            sconv[None], sssm[None])
```

```python
import functools

import numpy as np
import jax
import jax.numpy as jnp
from jax import lax
from jax.experimental import pallas as pl
from jax.experimental.pallas import tpu as pltpu

F32 = jnp.float32
BF16 = jnp.bfloat16
HIGHEST = lax.Precision.HIGHEST

RMS_EPS = 1e-6
LOG2E = 1.4426950408889634
SSD_HEAD_DIM = 64
SSD_N_GROUPS = 4
SSD_D_STATE = 128
SSD_CONV = 4
FOX_N_HEADS = 16
FOX_HEAD_DIM = 64
MEM_N_HEADS = 4
N_EXPERT_GROUPS = 4
EXPERTS_PER_GROUP = 8
N_EXPERTS = N_EXPERT_GROUPS * EXPERTS_PER_GROUP

LANES = 128
VMEM_LIMIT = 56 * 1024 * 1024


def _params(*sem):
    return pltpu.CompilerParams(dimension_semantics=sem, vmem_limit_bytes=VMEM_LIMIT)


def _dot(a, b, **kw):
    return jnp.dot(a, b, preferred_element_type=F32, **kw)


def _dot_nt(a, b, **kw):
    return lax.dot_general(a, b, (((1,), (1,)), ((), ())), preferred_element_type=F32, **kw)


def _dot_tn(a, b, **kw):
    return lax.dot_general(a, b, (((0,), (0,)), ((), ())), preferred_element_type=F32, **kw)


def _softplus(x):
    return jnp.maximum(x, 0.0) + jnp.log1p(jnp.exp(-jnp.abs(x)))


def _sigmoid(x):
    return 1.0 / (1.0 + jnp.exp(-x))


def _silu(x):
    return x * _sigmoid(x)


def _rms(x, g):
    return x * lax.rsqrt(jnp.mean(x * x, axis=-1, keepdims=True) + RMS_EPS) * g


def _split3(f):
    hi = f.astype(BF16).astype(F32)
    r = f - hi
    mid = r.astype(BF16).astype(F32)
    lo = (r - mid).astype(BF16).astype(F32)
    return hi, mid, lo


def _iota2(shape, dim):
    return lax.broadcasted_iota(jnp.int32, shape, dim)


def _rmsnorm_kernel(x_ref, g_ref, o_ref):
    o_ref[...] = _rms(x_ref[...].astype(F32), g_ref[...]).astype(o_ref.dtype)


def _rmsnorm(x, g, tm):
    m, d = x.shape
    return pl.pallas_call(
        _rmsnorm_kernel,
        grid=(m // tm,),
        in_specs=[pl.BlockSpec((tm, d), lambda i: (i, 0)), pl.BlockSpec((1, d), lambda i: (0, 0))],
        out_specs=pl.BlockSpec((tm, d), lambda i: (i, 0)),
        out_shape=jax.ShapeDtypeStruct((m, d), BF16),
        compiler_params=_params("parallel"),
        name="rmsnorm",
    )(x, g.reshape(1, d))


def _mm_kernel(a_ref, w_ref, o_ref, *, scale):
    acc = _dot(a_ref[...], w_ref[0])
    o_ref[...] = (acc * scale if scale != 1.0 else acc).astype(o_ref.dtype)


def _tile_cols(w, tn):
    k, n = w.shape
    return jnp.swapaxes(w.reshape(k, n // tn, tn), 0, 1)


def _matmul(a, w_tiles, out_dtype, *, tm, scale=1.0):
    m, k = a.shape
    nt, _, tn = w_tiles.shape
    return pl.pallas_call(
        functools.partial(_mm_kernel, scale=scale),
        grid=(m // tm, nt),
        in_specs=[pl.BlockSpec((tm, k), lambda i, j: (i, 0)), pl.BlockSpec((1, k, tn), lambda i, j: (j, 0, 0))],
        out_specs=pl.BlockSpec((tm, tn), lambda i, j: (i, j)),
        out_shape=jax.ShapeDtypeStruct((m, nt * tn), out_dtype),
        compiler_params=_params("parallel", "arbitrary"),
        name="matmul",
    )(a, w_tiles)


def _ssd_kernel(xbc_ref, z_ref, small_ref, dtT_ref, conv0_ref, ssm0_ref,
                convw_ref, convb_ref, dtb_ref, dtbc_ref, alog_ref, alogc_ref, dskip_ref, gnorm_ref, e_ref,
                y_ref, convn_ref, ssm_ref, convbuf, ybuf, *, T, CPS, valid, n_heads, cast):
    step = pl.program_id(1)
    d_inner = n_heads * SSD_HEAD_DIM
    gw = d_inner // SSD_N_GROUPS
    hpg = n_heads // SSD_N_GROUPS
    n = SSD_D_STATE
    k1 = SSD_CONV - 1

    @pl.when(step == 0)
    def _():
        ssm_ref[...] = ssm0_ref[...]
        convbuf[8 - k1:8, :] = conv0_ref[0]

    row_t = _iota2((T, T), 0)
    col_t = _iota2((T, T), 1)
    causal = row_t >= col_t
    lmat = causal.astype(F32)
    umat = (row_t <= col_t).astype(F32)
    lane_lo = _iota2((T, LANES), 1) < SSD_HEAD_DIM
    row_lo = _iota2((LANES, 1), 0) < SSD_HEAD_DIM
    a_row = -jnp.exp(alog_ref[...])
    a_col = -jnp.exp(alogc_ref[...])
    emat = e_ref[...]

    def chunk(c, carry):
        r0 = pl.multiple_of(c * T, T)
        raw = xbc_ref[0, pl.ds(r0, T), :]
        convbuf[8:8 + T, :] = raw
        acc = convb_ref[...] + convbuf[8 - k1:8 - k1 + T, :] * convw_ref[0:1, :]
        for j in range(1, SSD_CONV):
            acc = acc + convbuf[8 - k1 + j:8 - k1 + j + T, :] * convw_ref[j:j + 1, :]
        if valid is None:
            new_conv = raw[T - k1:T, :]
        else:
            new_conv = raw[valid - k1:valid, :]
        convbuf[8 - k1:8, :] = raw[T - k1:T, :]
        convn_ref[0] = new_conv
        xbc = _silu(acc)
        xs = xbc[:, :d_inner]
        bm = xbc[:, d_inner:d_inner + SSD_N_GROUPS * n]
        cm = xbc[:, d_inner + SSD_N_GROUPS * n:]

        dt = _softplus(small_ref[0, pl.ds(r0, T), :][:, :n_heads] + dtb_ref[...])
        dtT = _softplus(dtT_ref[0, c] + dtbc_ref[...])
        if valid is not None:
            dt = jnp.where(_iota2((T, n_heads), 0) < valid, dt, 0.0)
            dtT = jnp.where(_iota2((n_heads, T), 1) < valid, dtT, 0.0)
        a_cs = _dot(lmat, dt * a_row, precision=HIGHEST)
        a_csT = _dot(dtT * a_col, umat, precision=HIGHEST)
        eacs = jnp.exp(a_cs)
        dend = jnp.exp(a_cs[T - 1:T, :] - a_cs)
        expand = lambda x: _dot(jnp.concatenate(_split3(x), axis=1), emat)
        dt_x = expand(dt)
        dend_x = expand(dend)
        eacs_x = expand(eacs)
        xdt = xs * dt_x
        xte = xdt * dend_x

        for g in range(SSD_N_GROUPS):
            bg = cast(bm[:, g * n:(g + 1) * n])
            cg = cast(cm[:, g * n:(g + 1) * n])
            cb = _dot_nt(cg, bg)
            for jp in range(hpg // 2):
                h1 = g * hpg + 2 * jp
                lo = h1 * SSD_HEAD_DIM
                xdt_p = cast(xdt[:, lo:lo + LANES])
                yd = []
                for h in (h1, h1 + 1):
                    seg = a_cs[:, h:h + 1] - a_csT[h:h + 1, :]
                    dec = jnp.exp(jnp.where(causal, seg, -jnp.inf))
                    yd.append(_dot(cast(cb * dec), xdt_p))
                y_diag = jnp.where(lane_lo, yd[0], yd[1])
                st = ssm_ref[0, h1:h1 + 2].reshape(2 * SSD_HEAD_DIM, n)
                y_off = _dot_nt(cg, cast(st)) * eacs_x[:, lo:lo + LANES]
                cs = _dot_tn(cast(xte[:, lo:lo + LANES]), bg)
                dl = jnp.where(row_lo, eacs[T - 1:T, h1:h1 + 1], eacs[T - 1:T, h1 + 1:h1 + 2])
                ssm_ref[0, h1:h1 + 2] = (st * dl + cs).reshape(2, SSD_HEAD_DIM, n)
                ybuf[:, lo:lo + LANES] = (y_diag + y_off) + dskip_ref[:, lo:lo + LANES] * xs[:, lo:lo + LANES]

        zz = z_ref[0, pl.ds(r0, T), :]
        hg = ybuf[...] * _silu(zz)
        for g in range(SSD_N_GROUPS):
            hgg = hg[:, g * gw:(g + 1) * gw]
            y_ref[0, pl.ds(r0, T), g * gw:(g + 1) * gw] = _rms(hgg, gnorm_ref[:, g * gw:(g + 1) * gw]).astype(y_ref.dtype)
        return carry

    lax.fori_loop(0, CPS, chunk, 0)


def _ssd(xbc, z, small, conv0, ssm0, conv_w, conv_b, dt_bias, a_log, d_skip, ssd_norm, *, T, CPS, valid):
    b, l, cdim = xbc.shape
    d_inner = z.shape[-1]
    nh = d_inner // SSD_HEAD_DIM
    rows = T * CPS
    nsteps = l // rows
    assert l % rows == 0 and (valid is None or (nsteps == 1 and CPS == 1 and valid >= SSD_CONV - 1))
    dtT = jnp.swapaxes(small[:, :, :nh].reshape(b, l // T, T, nh), 2, 3)
    emat = jnp.asarray(np.tile(np.repeat(np.eye(nh, dtype=np.float32), SSD_HEAD_DIM, axis=1), (3, 1)))
    dskip_x = jnp.repeat(d_skip.astype(F32), SSD_HEAD_DIM).reshape(1, d_inner)
    aligned = T % 16 == 0
    cast = (lambda v: v.astype(BF16)) if aligned else (lambda v: v)
    full = lambda shape: pl.BlockSpec(shape, lambda i, s: (0,) * len(shape))
    y, convn, ssmn = pl.pallas_call(
        functools.partial(_ssd_kernel, T=T, CPS=CPS, valid=valid, n_heads=nh, cast=cast),
        grid=(b, nsteps),
        in_specs=[
            pl.BlockSpec((1, rows, cdim), lambda i, s: (i, s, 0)),
            pl.BlockSpec((1, rows, d_inner), lambda i, s: (i, s, 0)),
            pl.BlockSpec((1, rows, LANES), lambda i, s: (i, s, 0)),
            pl.BlockSpec((1, CPS, nh, T), lambda i, s: (i, s, 0, 0)),
            pl.BlockSpec((1, SSD_CONV - 1, cdim), lambda i, s: (i, 0, 0)),
            pl.BlockSpec((1, nh, SSD_HEAD_DIM, SSD_D_STATE), lambda i, s: (i, 0, 0, 0)),
            full((SSD_CONV, cdim)), full((1, cdim)), full((1, nh)), full((nh, 1)), full((1, nh)), full((nh, 1)),
            full((1, d_inner)), full((1, d_inner)), full((3 * nh, d_inner)),
        ],
        out_specs=[
            pl.BlockSpec((1, rows, d_inner), lambda i, s: (i, s, 0)),
            pl.BlockSpec((1, SSD_CONV - 1, cdim), lambda i, s: (i, 0, 0)),
            pl.BlockSpec((1, nh, SSD_HEAD_DIM, SSD_D_STATE), lambda i, s: (i, 0, 0, 0)),
        ],
        out_shape=[
            jax.ShapeDtypeStruct((b, l, d_inner), BF16 if aligned else F32),
            jax.ShapeDtypeStruct((b, SSD_CONV - 1, cdim), F32),
            jax.ShapeDtypeStruct((b, nh, SSD_HEAD_DIM, SSD_D_STATE), F32),
        ],
        scratch_shapes=[pltpu.VMEM((8 + T, cdim), F32), pltpu.VMEM((T, d_inner), F32)],
        compiler_params=_params("parallel", "arbitrary"),
        name="ssd",
    )(xbc, z, small, dtT, conv0, ssm0, conv_w, conv_b.reshape(1, cdim), dt_bias.reshape(1, nh), dt_bias.reshape(nh, 1),
      a_log.reshape(1, nh), a_log.reshape(nh, 1), dskip_x, ssd_norm.reshape(1, d_inner), emat)
    return y, convn, ssmn


_AUG = 3


def _fox_gate_kernel(small_ref, q_ref, k_ref, bf_ref, pq_ref, pk_ref, cq_ref, ck_ref,
                     logf_ref, qa_ref, ka_ref, carry, *, tr, f_off):
    nh = FOX_N_HEADS

    @pl.when(pl.program_id(1) == 0)
    def _():
        carry[...] = jnp.zeros_like(carry)

    logf = -_softplus(-(small_ref[0][:, f_off:f_off + nh] + bf_ref[...]))
    logf_ref[0] = logf
    lmat = (_iota2((tr, tr), 0) >= _iota2((tr, tr), 1)).astype(F32)
    fcum = _dot(lmat, logf, precision=HIGHEST) + carry[...]
    carry[...] = fcum[tr - 1:tr, :]
    pieces = _split3(fcum * LOG2E)
    xq = cq_ref[...]
    xk = ck_ref[...]
    for i, p in enumerate(pieces):
        xq = xq + _dot(p, pq_ref[i])
        xk = xk + _dot(p, pk_ref[i])
    lane = _iota2((tr, LANES), 1)
    for h in range(nh):
        sel = (lane < FOX_HEAD_DIM) if h % 2 == 0 else (lane >= FOX_HEAD_DIM)
        pair = slice((h // 2) * LANES, (h // 2 + 1) * LANES)
        blk = slice(h * LANES, (h + 1) * LANES)
        qa_ref[0, h] = jnp.where(sel, q_ref[0][:, pair].astype(F32), xq[:, blk]).astype(BF16)
        ka_ref[0, h] = jnp.where(sel, k_ref[0][:, pair], xk[:, blk]).astype(BF16)


def _aug_tables():
    nh = FOX_N_HEADS
    pq = np.zeros((_AUG, nh, nh * LANES), np.float32)
    pk = np.zeros((_AUG, nh, nh * LANES), np.float32)
    cq = np.zeros((1, nh * LANES), np.float32)
    ck = np.zeros((1, nh * LANES), np.float32)
    for h in range(nh):
        off = h * LANES + (FOX_HEAD_DIM if h % 2 == 0 else 0)
        for i in range(_AUG):
            pq[i, h, off + i] = 1.0
            ck[0, off + i] = 1.0
            cq[0, off + _AUG + i] = 1.0
            pk[i, h, off + _AUG + i] = -1.0
    return jnp.asarray(pq), jnp.asarray(pk), jnp.asarray(cq), jnp.asarray(ck)


def _fox_gate(small, q, k, b_fox_f, *, tr, f_off):
    b, l, _ = small.shape
    nh = FOX_N_HEADS
    pq, pk, cq, ck = _aug_tables()
    full = lambda shape: pl.BlockSpec(shape, lambda i, s: (0,) * len(shape))
    return pl.pallas_call(
        functools.partial(_fox_gate_kernel, tr=tr, f_off=f_off),
        grid=(b, l // tr),
        in_specs=[
            pl.BlockSpec((1, tr, LANES), lambda i, s: (i, s, 0)),
            pl.BlockSpec((1, tr, nh * FOX_HEAD_DIM), lambda i, s: (i, s, 0)),
            pl.BlockSpec((1, tr, nh * FOX_HEAD_DIM), lambda i, s: (i, s, 0)),
            full((1, nh)), full((_AUG, nh, nh * LANES)), full((_AUG, nh, nh * LANES)),
            full((1, nh * LANES)), full((1, nh * LANES)),
        ],
        out_specs=[
            pl.BlockSpec((1, tr, nh), lambda i, s: (i, s, 0)),
            pl.BlockSpec((1, nh, tr, LANES), lambda i, s: (i, 0, s, 0)),
            pl.BlockSpec((1, nh, tr, LANES), lambda i, s: (i, 0, s, 0)),
        ],
        out_shape=[
            jax.ShapeDtypeStruct((b, l, nh), F32),
            jax.ShapeDtypeStruct((b, nh, l, LANES), BF16),
            jax.ShapeDtypeStruct((b, nh, l, LANES), BF16),
        ],
        scratch_shapes=[pltpu.VMEM((1, nh), F32)],
        compiler_params=_params("parallel", "arbitrary"),
        name="fox_gate",
    )(small, q, k, b_fox_f.reshape(1, nh), pq, pk, cq, ck)


def _fox_prompt_kernel(qi_ref, ki_ref, q_ref, k_ref, v_ref, o_ref, m_scr, acc_scr, *, tq, hps):
    s_idx = pl.program_id(2)
    qi = qi_ref[s_idx]
    ki = ki_ref[s_idx]

    @pl.when(ki == 0)
    def _():
        m_scr[...] = jnp.full_like(m_scr, -jnp.inf)
        acc_scr[...] = jnp.zeros_like(acc_scr)

    def body(masked):
        reps = tq // LANES
        lane_lo = _iota2((tq, LANES), 1) < FOX_HEAD_DIM
        if masked:
            keep = _iota2((tq, tq), 1) <= _iota2((tq, tq), 0)
        for hh in range(hps):
            pair = slice((hh // 2) * LANES, (hh // 2 + 1) * LANES)
            own = lane_lo if hh % 2 == 0 else jnp.logical_not(lane_lo)
            v1 = jnp.where(own, v_ref[0, :, pair], 1.0).astype(BF16)
            s = _dot_nt(q_ref[0, hh], k_ref[0, hh])
            if masked:
                s = jnp.where(keep, s, -jnp.inf)
            m_prev = m_scr[hh]
            m_new = jnp.maximum(m_prev, jnp.max(s, axis=1, keepdims=True))
            p = jnp.exp2(s - jnp.tile(m_new, (1, reps)))
            acc_scr[hh] = jnp.exp2(m_prev - m_new) * acc_scr[hh] + _dot(p.astype(BF16), v1)
            m_scr[hh] = m_new

    pl.when(ki < qi)(lambda: body(False))

    @pl.when(ki == qi)
    def _():
        body(True)
        lane_lo = _iota2((tq, LANES), 1) < FOX_HEAD_DIM
        for pp in range(hps // 2):
            a0, a1 = acc_scr[2 * pp], acc_scr[2 * pp + 1]
            o0 = a0 / pltpu.roll(a0, FOX_HEAD_DIM, axis=1)
            o1 = a1 / pltpu.roll(a1, FOX_HEAD_DIM, axis=1)
            o_ref[0, :, pp * LANES:(pp + 1) * LANES] = jnp.where(lane_lo, o0, o1).astype(o_ref.dtype)


def _fox_prompt(q_aug, k_aug, v, *, tq, hps):
    b, nh, l, _ = q_aug.shape
    nq = l // tq
    vw = hps * FOX_HEAD_DIM
    qi = np.array([i for i in range(nq) for _ in range(i + 1)], np.int32)
    ki = np.array([j for i in range(nq) for j in range(i + 1)], np.int32)
    grid_spec = pltpu.PrefetchScalarGridSpec(
        num_scalar_prefetch=2,
        grid=(b, nh // hps, len(qi)),
        in_specs=[
            pl.BlockSpec((1, hps, tq, LANES), lambda i, p, s, qi, ki: (i, p, qi[s], 0)),
            pl.BlockSpec((1, hps, tq, LANES), lambda i, p, s, qi, ki: (i, p, ki[s], 0)),
            pl.BlockSpec((1, tq, vw), lambda i, p, s, qi, ki: (i, ki[s], p)),
        ],
        out_specs=pl.BlockSpec((1, tq, vw), lambda i, p, s, qi, ki: (i, qi[s], p)),
        scratch_shapes=[pltpu.VMEM((hps, tq, LANES), F32), pltpu.VMEM((hps, tq, LANES), F32)],
    )
    return pl.pallas_call(
        functools.partial(_fox_prompt_kernel, tq=tq, hps=hps),
        grid_spec=grid_spec,
        out_shape=jax.ShapeDtypeStruct((b, l, nh * FOX_HEAD_DIM), BF16),
        compiler_params=_params("parallel", "parallel", "arbitrary"),
        name="fox_prompt",
    )(jnp.asarray(qi), jnp.asarray(ki), q_aug, k_aug, v)


def _gate_scan_kernel(lf_ref, u_ref, o_ref):
    o_ref[...] = _dot(jnp.concatenate(_split3(lf_ref[...]), axis=1), u_ref[...])


def _gate_scan(lf_rows, *, rows):
    n, page = lf_rows.shape
    later = np.triu(np.ones((page, page), np.float32), 1).T
    u3 = jnp.asarray(np.tile(np.concatenate([later, np.ones((page, page), np.float32)], axis=1), (3, 1)))
    return pl.pallas_call(
        _gate_scan_kernel,
        grid=(n // rows,),
        in_specs=[pl.BlockSpec((rows, page), lambda i: (i, 0)), pl.BlockSpec((3 * page, 2 * page), lambda i: (0, 0))],
        out_specs=pl.BlockSpec((rows, 2 * page), lambda i: (i, 0)),
        out_shape=jax.ShapeDtypeStruct((n, 2 * page), F32),
        compiler_params=_params("parallel"),
        name="gate_scan",
    )(lf_rows, u3)


def _fox_sample_kernel(pt_ref, q_ref, kn_ref, vn_ref, small_ref, bf_ref, *rest, PG, page, nq, f_off):
    kt_refs = rest[:PG]
    vt_refs = rest[PG:2 * PG]
    gl_refs = rest[2 * PG:3 * PG]
    o_ref, logf_ref, qx_scr, m_scr, l_scr, acc_scr, carry_scr = rest[3 * PG:]
    nh, dh = FOX_N_HEADS, FOX_HEAD_DIM
    rows = nq * nh
    width = nh * dh
    g = pl.program_id(1)
    own = (_iota2((rows, width), 1) // dh) == (_iota2((rows, width), 0) % nh)

    logf_new = -_softplus(-(small_ref[0][:, f_off:f_off + nh] + bf_ref[...]))
    lf8 = jnp.where(_iota2((8, nh), 0) < nq, logf_new, 0.0)
    tri8 = (_iota2((8, 8), 0) <= _iota2((8, 8), 1)).astype(F32)
    fnew = _dot_tn(tri8, lf8, precision=HIGHEST)
    fnewT = _dot_tn(lf8, tri8, precision=HIGHEST)
    frow = jnp.concatenate([jnp.broadcast_to(fnew[qq:qq + 1, :], (nh, nh)) for qq in range(nq)], axis=0)
    fcol = jnp.sum(jnp.where(_iota2((rows, nh), 1) == _iota2((rows, nh), 0) % nh, frow, 0.0), axis=1, keepdims=True)

    @pl.when(g == 0)
    def _():
        qrows = jnp.concatenate([jnp.broadcast_to(q_ref[0][qq:qq + 1, :], (nh, width)) for qq in range(nq)], axis=0)
        qx_scr[...] = jnp.where(own, qrows, 0.0)
        m_scr[...] = jnp.full_like(m_scr, -jnp.inf)
        l_scr[...] = jnp.zeros_like(l_scr)
        acc_scr[...] = jnp.zeros_like(acc_scr)
        carry_scr[...] = jnp.zeros_like(carry_scr)
        logf_ref[0] = logf_new

    qx = qx_scr[...]

    scores = []
    carry = carry_scr[...]
    for i in range(PG):
        gl = gl_refs[i][...]
        gate = carry + gl[:, :page]
        carry = carry + gl[:, page:]
        scores.append(_dot(qx, kt_refs[i][0]) + fcol + jnp.concatenate([gate] * nq, axis=0))
    carry_scr[...] = carry
    m_prev = m_scr[...]
    m_new = m_prev
    for s in scores:
        m_new = jnp.maximum(m_new, jnp.max(s, axis=1, keepdims=True))
    alpha = jnp.exp(m_prev - m_new)
    l_new = alpha * l_scr[...]
    acc = alpha * acc_scr[...]
    for i, s in enumerate(scores):
        p = jnp.exp(s - m_new)
        l_new = l_new + jnp.sum(p, axis=1, keepdims=True)
        acc = acc + _dot_nt(p, vt_refs[i][0])
    m_scr[...] = m_new
    l_scr[...] = l_new
    acc_scr[...] = acc

    @pl.when(g == pl.num_programs(1) - 1)
    def _():
        s = _dot_nt(qx, kn_ref[0])
        s = s + fcol - jnp.concatenate([fnewT] * nq, axis=0)
        keep = _iota2((rows, 8), 1) <= _iota2((rows, 8), 0) // nh
        s = jnp.where(keep, s, -jnp.inf)
        m_fin = jnp.maximum(m_scr[...], jnp.max(s, axis=1, keepdims=True))
        a_fin = jnp.exp(m_scr[...] - m_fin)
        p = jnp.exp(s - m_fin)
        l_fin = a_fin * l_scr[...] + jnp.sum(p, axis=1, keepdims=True)
        out = jnp.where(own, (a_fin * acc_scr[...] + _dot(p, vn_ref[0])) / l_fin, 0.0)
        pick = (_iota2((8, rows), 1) // nh == _iota2((8, rows), 0)).astype(F32)
        o_ref[0] = _dot(pick, out, precision=HIGHEST).astype(o_ref.dtype)


def _fox_sample(q, k_new, v_new, small, b_fox_f, cache_kt, cache_vt, gates, page_table, *, nq, PG, f_off):
    b, _, width = q.shape
    nh = FOX_N_HEADS
    n_pages = page_table.shape[1]
    page = cache_kt.shape[2]
    assert n_pages % PG == 0
    rows = nq * nh

    def pmap(i):
        return lambda bb, g, pt: (pt[bb, n_pages - 1 - (g * PG + i)], 0, 0)

    def gmap(i):
        return lambda bb, g, pt: (pt[bb, n_pages - 1 - (g * PG + i)], 0)

    fixed = lambda shape: pl.BlockSpec(shape, lambda bb, g, pt: (bb,) + (0,) * (len(shape) - 1))
    in_specs = [fixed((1, 8, width)), fixed((1, 8, width)), fixed((1, 8, width)), fixed((1, 8, LANES)),
                pl.BlockSpec((1, nh), lambda bb, g, pt: (0, 0))]
    in_specs += [pl.BlockSpec((1, width, page), pmap(i)) for i in range(PG)]
    in_specs += [pl.BlockSpec((1, width, page), pmap(i)) for i in range(PG)]
    in_specs += [pl.BlockSpec((nh, 2 * page), gmap(i)) for i in range(PG)]
    grid_spec = pltpu.PrefetchScalarGridSpec(
        num_scalar_prefetch=1,
        grid=(b, n_pages // PG),
        in_specs=in_specs,
        out_specs=[fixed((1, 8, width)), fixed((1, 8, nh))],
        scratch_shapes=[pltpu.VMEM((rows, width), F32), pltpu.VMEM((rows, 1), F32), pltpu.VMEM((rows, 1), F32),
                        pltpu.VMEM((rows, width), F32), pltpu.VMEM((nh, page), F32)],
    )
    return pl.pallas_call(
        functools.partial(_fox_sample_kernel, PG=PG, page=page, nq=nq, f_off=f_off),
        grid_spec=grid_spec,
        out_shape=[jax.ShapeDtypeStruct((b, 8, width), F32), jax.ShapeDtypeStruct((b, 8, nh), F32)],
        compiler_params=_params("parallel", "arbitrary"),
        name="fox_sample",
    )(page_table, q, k_new, v_new, small, b_fox_f.reshape(1, nh),
      *([cache_kt] * PG), *([cache_vt] * PG), *([gates] * PG))


def _mix_kernel(x_ref, ys_ref, yf_ref, gt_ref, bg_ref, wso_ref, wfo_ref, wmx_ref, nm_ref, wq_ref, x1_ref, qm_ref, *, qscale):
    d = x_ref.shape[-1]
    gate = _sigmoid(gt_ref[...] + bg_ref[...])
    merged = gate[:, :d] * _dot(ys_ref[...], wso_ref[...]) + gate[:, d:] * _dot(yf_ref[...], wfo_ref[...])
    x1 = x_ref[...] + _dot(merged.astype(BF16), wmx_ref[...])
    x1_ref[...] = x1
    h = _rms(x1, nm_ref[...]).astype(BF16)
    qm_ref[...] = (_dot(h, wq_ref[...]) * qscale).astype(qm_ref.dtype)


def _mix(x, y_ssd, y_fox, gates, b_gate, w_so, w_fo, w_mx, norm_mem, w_q, *, tm, qscale, q_dtype):
    m, d = x.shape
    row = lambda w: pl.BlockSpec((tm, w), lambda i: (i, 0))
    full = lambda a: pl.BlockSpec(a.shape, lambda i: (0,) * a.ndim)
    bg, nm = b_gate.reshape(1, -1), norm_mem.reshape(1, -1)
    return pl.pallas_call(
        functools.partial(_mix_kernel, qscale=qscale),
        grid=(m // tm,),
        in_specs=[row(d), row(y_ssd.shape[1]), row(d), row(2 * d), full(bg), full(w_so), full(w_fo), full(w_mx), full(nm), full(w_q)],
        out_specs=[row(d), row(d)],
        out_shape=[jax.ShapeDtypeStruct((m, d), F32), jax.ShapeDtypeStruct((m, d), q_dtype)],
        compiler_params=_params("parallel"),
        name="mix",
    )(x, y_ssd, y_fox, gates, bg, w_so, w_fo, w_mx, nm, w_q)


def _mem_attn_kernel(q_ref, k_ref, v_ref, o_ref, *, cast):
    dh = q_ref.shape[-1] // MEM_N_HEADS
    for h in range(MEM_N_HEADS):
        cols = slice(h * dh, (h + 1) * dh)
        s = _dot_nt(cast(q_ref[0][:, cols]), cast(k_ref[0][:, cols]))
        p = jnp.exp(s - jnp.max(s, axis=1, keepdims=True))
        p = p / jnp.sum(p, axis=1, keepdims=True)
        o_ref[0, :, cols] = _dot(cast(p), cast(v_ref[0][:, cols])).astype(o_ref.dtype)


def _mem_attn(q, mem_k, mem_v, *, tl):
    b, l, d = q.shape
    m = mem_k.shape[1]
    cast = (lambda v: v.astype(BF16)) if q.dtype == BF16 else (lambda v: v)
    return pl.pallas_call(
        functools.partial(_mem_attn_kernel, cast=cast),
        grid=(b, l // tl),
        in_specs=[pl.BlockSpec((1, tl, d), lambda i, s: (i, s, 0)),
                  pl.BlockSpec((1, m, d), lambda i, s: (i, 0, 0)),
                  pl.BlockSpec((1, m, d), lambda i, s: (i, 0, 0))],
        out_specs=pl.BlockSpec((1, tl, d), lambda i, s: (i, s, 0)),
        out_shape=jax.ShapeDtypeStruct((b, l, d), q.dtype),
        compiler_params=_params("parallel", "parallel"),
        name="mem_attn",
    )(q, mem_k, mem_v)


def _route_kernel(x1_ref, att_ref, wo_ref, nf_ref, wr_ref, br_ref, x2_ref, h_ref, rt_ref):
    x2 = x1_ref[...] + _dot(att_ref[...], wo_ref[...])
    x2_ref[...] = x2
    h = _rms(x2, nf_ref[...])
    h_ref[...] = h.astype(h_ref.dtype)
    logits = _dot(h, wr_ref[...], precision=HIGHEST) + br_ref[...]
    lane = _iota2(logits.shape, 1).astype(F32)
    first = lambda mask: jnp.min(jnp.where(mask, lane, float(LANES)), axis=1, keepdims=True)
    gl = jnp.where(lane < N_EXPERT_GROUPS, logits, -jnp.inf)
    gmax = jnp.max(gl, axis=1, keepdims=True)
    g_idx = first(gl == gmax)
    g_w = 1.0 / jnp.sum(jnp.exp(gl - gmax), axis=1, keepdims=True)
    e_lo = N_EXPERT_GROUPS + g_idx * EXPERTS_PER_GROUP
    el = jnp.where((lane >= e_lo) & (lane < e_lo + EXPERTS_PER_GROUP), logits, -jnp.inf)
    v1 = jnp.max(el, axis=1, keepdims=True)
    i1 = first(el == v1)
    el2 = jnp.where(lane == i1, -jnp.inf, el)
    v2 = jnp.max(el2, axis=1, keepdims=True)
    i2 = first(el2 == v2)
    e21 = jnp.exp(v2 - v1)
    w1 = g_w / (1.0 + e21)
    w2 = g_w * e21 / (1.0 + e21)
    rt_ref[...] = jnp.where(lane == 0, i1 - N_EXPERT_GROUPS, jnp.where(lane == 1, i2 - N_EXPERT_GROUPS,
                            jnp.where(lane == 2, w1, jnp.where(lane == 3, w2, 0.0))))


def _route(x1, att, w_o, norm_ffn, w_router, b_router, *, tm):
    m, d = x1.shape
    row = lambda w: pl.BlockSpec((tm, w), lambda i: (i, 0))
    full = lambda a: pl.BlockSpec(a.shape, lambda i: (0,) * a.ndim)
    nf = norm_ffn.reshape(1, d)
    return pl.pallas_call(
        _route_kernel,
        grid=(m // tm,),
        in_specs=[row(d), row(d), full(w_o), full(nf), full(w_router), full(b_router)],
        out_specs=[row(d), row(d), row(LANES)],
        out_shape=[jax.ShapeDtypeStruct((m, d), F32), jax.ShapeDtypeStruct((m, d), BF16), jax.ShapeDtypeStruct((m, LANES), F32)],
        compiler_params=_params("parallel"),
        name="route",
    )(x1, att, w_o, nf, w_router, b_router)


def _expert_kernel(te_ref, nt_ref, x_ref, wg_ref, wu_ref, wd_ref, o_ref):
    @pl.when(pl.program_id(0) < nt_ref[0])
    def _():
        x = x_ref[...]
        hid = _silu(_dot(x, wg_ref[0].astype(BF16))) * _dot(x, wu_ref[0].astype(BF16))
        o_ref[...] = _dot(hid.astype(BF16), wd_ref[0].astype(BF16)).astype(o_ref.dtype)

    @pl.when(pl.program_id(0) >= nt_ref[0])
    def _():
        o_ref[...] = jnp.zeros_like(o_ref)


def _experts(x_sorted, tile_expert, n_tiles, w_gate, w_up, w_down, *, tm):
    p, d = x_sorted.shape
    ne, _, ff = w_gate.shape
    grid_spec = pltpu.PrefetchScalarGridSpec(
        num_scalar_prefetch=2,
        grid=(p // tm,),
        in_specs=[pl.BlockSpec((tm, d), lambda i, te, nt: (i, 0)),
                  pl.BlockSpec((1, d, ff), lambda i, te, nt: (te[i], 0, 0)),
                  pl.BlockSpec((1, d, ff), lambda i, te, nt: (te[i], 0, 0)),
                  pl.BlockSpec((1, ff, d), lambda i, te, nt: (te[i], 0, 0))],
        out_specs=pl.BlockSpec((tm, d), lambda i, te, nt: (i, 0)),
    )
    return pl.pallas_call(
        _expert_kernel,
        grid_spec=grid_spec,
        out_shape=jax.ShapeDtypeStruct((p, d), F32),
        compiler_params=_params("arbitrary"),
        name="experts",
    )(tile_expert, n_tiles, x_sorted, w_gate, w_up, w_down)


def _final_kernel(x2_ref, ya_ref, yb_ref, rt_ref, g_ref, o_ref):
    rt = rt_ref[...]
    moe = rt[:, 2:3] * ya_ref[...] + rt[:, 3:4] * yb_ref[...]
    o_ref[...] = _rms(x2_ref[...] + moe, g_ref[...])


def _final(x2, ya, yb, rt, norm_final, *, tm):
    m, d = x2.shape
    row = lambda w: pl.BlockSpec((tm, w), lambda i: (i, 0))
    return pl.pallas_call(
        _final_kernel,
        grid=(m // tm,),
        in_specs=[row(d), row(d), row(d), row(LANES), pl.BlockSpec((1, d), lambda i: (0, 0))],
        out_specs=row(d),
        out_shape=jax.ShapeDtypeStruct((m, d), F32),
        compiler_params=_params("parallel"),
        name="final",
    )(x2, ya, yb, rt, norm_final.reshape(1, d))


def _dispatch_plan(eid, tm):
    n = eid.shape[0]
    p = (n + N_EXPERTS * (tm - 1) + tm - 1) // tm * tm
    onehot = (eid[:, None] == jnp.arange(N_EXPERTS, dtype=jnp.int32)[None, :]).astype(jnp.int32)
    csum = jnp.cumsum(onehot, axis=0)
    counts = csum[-1]
    rank = jnp.sum(csum * onehot, axis=1) - 1
    padded = (counts + tm - 1) // tm * tm
    pend = jnp.cumsum(padded)
    pstart = pend - padded
    pos = jnp.sum(pstart[None, :] * onehot, axis=1) + rank
    src = jnp.full((p,), -1, jnp.int32).at[pos].set(jnp.arange(n, dtype=jnp.int32), unique_indices=True)
    n_tiles = (pend[-1] // tm).astype(jnp.int32)
    tile_first = jnp.arange(p // tm, dtype=jnp.int32) * tm
    tile_expert = jnp.sum((tile_first[:, None] >= pend[None, :]).astype(jnp.int32), axis=1)
    last_used = jnp.sum((jnp.maximum(pend[-1] - tm, 0) >= pend).astype(jnp.int32))
    tile_expert = jnp.where(tile_first < pend[-1], tile_expert, last_used).astype(jnp.int32)
    return src, pos, tile_expert, n_tiles.reshape(1)


def _bf(w):
    return w.astype(BF16)


def kernel(x_prompt, x_sample, mem_prompt, cache_fox_k, cache_fox_v, cache_fox_logf, page_table, cache_mem_k, cache_mem_v, state_conv, state_ssm, norm_mix, w_in, conv_w, conv_b, dt_bias, a_log, d_skip, ssd_norm, w_ssd_out, b_fox_f, w_fox_out, b_gate, w_mix_out, norm_mem, norm_mem_kv, w_mem_q, w_mem_k, w_mem_v, w_mem_o, norm_ffn, w_router_group, b_router_group, w_router_expert, b_router_expert, w_exp_gate, w_exp_up, w_exp_down, norm_final):
    depth = w_in.shape[0]
    assert depth == 1, "single-layer step"
    bp, sp, d = x_prompt.shape
    bs, ss, _ = x_sample.shape
    tp, ts = bp * sp, bs * ss
    d_inner = ssd_norm.shape[-1]
    cdim = conv_w.shape[-1]
    nh_ssd = dt_bias.shape[-1]
    nh, dh = FOX_N_HEADS, FOX_HEAD_DIM
    fw = nh * dh
    mem_len = mem_prompt.shape[1]
    l = 0

    o_z, o_x, o_dt, o_q, o_k, o_v, o_f, o_g = np.cumsum([0, d_inner, cdim, nh_ssd, fw, fw, fw, nh]).tolist()
    wi = w_in[l]
    tn = 1024
    seg = lambda a, b: _tile_cols(_bf(wi[:, a:b]), tn)
    w_z, w_x, w_qf, w_kf, w_vf, w_g = seg(o_z, o_x), seg(o_x, o_dt), seg(o_q, o_k), seg(o_k, o_v), seg(o_v, o_f), seg(o_g, o_g + 2 * d)
    n_small = nh_ssd + nh
    w_small = _bf(jnp.pad(jnp.concatenate([wi[:, o_dt:o_q], wi[:, o_f:o_g]], axis=1), ((0, 0), (0, LANES - n_small))))[None]
    f_off = nh_ssd
    w_so, w_fo, w_mx, w_q, w_o = _bf(w_ssd_out[l]), _bf(w_fox_out[l]), _bf(w_mix_out[l]), _bf(w_mem_q[l]), _bf(w_mem_o[l])
    n_r = N_EXPERT_GROUPS + N_EXPERTS
    w_router = jnp.pad(jnp.concatenate([w_router_group[l], w_router_expert[l]], axis=1), ((0, 0), (0, LANES - n_r)))
    b_router = jnp.pad(jnp.concatenate([b_router_group[l], b_router_expert[l]]), (0, LANES - n_r)).reshape(1, LANES)
    mem_scale = (d // MEM_N_HEADS) ** -0.5

    def in_proj(x2d, tm, q_dtype, q_scale):
        h = _rmsnorm(x2d, norm_mix[l], tm)
        mm = lambda w, dt=F32, sc=1.0: _matmul(h, w, dt, tm=tm, scale=sc)
        return mm(w_z), mm(w_x), mm(w_qf, q_dtype, q_scale), mm(w_kf), mm(w_vf), mm(w_g), mm(w_small)

    def post(x2d, y_ssd, y_fox, gates, mem_k, mem_v, nb, tm, tl):
        per = x2d.shape[0] // nb
        q_dtype = BF16 if per % 16 == 0 else F32
        x1, qm = _mix(x2d, y_ssd, y_fox, gates, b_gate[l], w_so, w_fo, w_mx, norm_mem[l], w_q, tm=tm, qscale=mem_scale, q_dtype=q_dtype)
        qm = qm.reshape(nb, per, d)
        if per % 8:
            qm = jnp.pad(qm, ((0, 0), (0, 8 - per % 8), (0, 0)))
        att = _mem_attn(qm, mem_k, mem_v, tl=tl)[:, :per].reshape(-1, d).astype(BF16)
        return _route(x1, att, w_o, norm_ffn[l], w_router, b_router, tm=tm)

    xp = x_prompt.reshape(tp, d)
    z, xbc, q, k, v, gates, small = in_proj(xp, 1024, BF16, dh ** -0.5 * LOG2E)
    conv0 = jnp.zeros((bp, SSD_CONV - 1, cdim), F32)
    ssm0 = jnp.zeros((bp, nh_ssd, SSD_HEAD_DIM, SSD_D_STATE), F32)
    y_ssd, pconv, pssm = _ssd(xbc.reshape(bp, sp, cdim), z.reshape(bp, sp, d_inner), small.reshape(bp, sp, LANES), conv0, ssm0,
                              conv_w[l], conv_b[l], dt_bias[l], a_log[l], d_skip[l], ssd_norm[l], T=128, CPS=4, valid=None)
    plogf, q_aug, k_aug = _fox_gate(small.reshape(bp, sp, LANES), q.reshape(bp, sp, fw), k.reshape(bp, sp, fw), b_fox_f[l], tr=512, f_off=f_off)
    y_fox = _fox_prompt(q_aug, k_aug, v.reshape(bp, sp, fw), tq=1024, hps=4)
    mem_h = _rmsnorm(mem_prompt.reshape(bp * mem_len, d), norm_mem_kv[l], 256)
    mk = _matmul(mem_h, _tile_cols(_bf(w_mem_k[l]), 512), F32, tm=256)
    mv = _matmul(mem_h, _tile_cols(_bf(w_mem_v[l]), 512), F32, tm=256)
    x2p, hp, rtp = post(xp, y_ssd.reshape(tp, d_inner), y_fox.reshape(tp, fw), gates, mk.reshape(bp, mem_len, d), mv.reshape(bp, mem_len, d), bp, 256, 512)

    xs = x_sample.reshape(ts, d)
    zs, xbcs, qs, ks, vs, gates_s, small_s = in_proj(xs, ts, F32, dh ** -0.5)
    pad8 = lambda a: jnp.pad(a.reshape(bs, ss, -1), ((0, 0), (0, 8 - ss), (0, 0)))
    ys_ssd, sconv, sssm = _ssd(pad8(xbcs), pad8(zs), pad8(small_s), state_conv[l], state_ssm[l],
                               conv_w[l], conv_b[l], dt_bias[l], a_log[l], d_skip[l], ssd_norm[l], T=8, CPS=1, valid=ss)
    pool, page = cache_fox_k.shape[1], cache_fox_k.shape[2]
    kt = jnp.transpose(cache_fox_k[l], (0, 2, 3, 1)).reshape(pool, fw, page)
    vt = jnp.transpose(cache_fox_v[l], (0, 2, 3, 1)).reshape(pool, fw, page)
    gates_past = _gate_scan(jnp.transpose(cache_fox_logf[l], (0, 2, 1)).reshape(pool * nh, page), rows=2048)
    ys_fox, slogf = _fox_sample(pad8(qs), pad8(ks), pad8(vs), pad8(small_s), b_fox_f[l], kt, vt, gates_past, page_table,
                                nq=ss, PG=16, f_off=f_off)
    x2s, hs, rts = post(xs, ys_ssd[:, :ss].reshape(ts, d_inner).astype(BF16), ys_fox[:, :ss].reshape(ts, fw).astype(BF16), gates_s,
                        cache_mem_k[l].reshape(bs, mem_len, d), cache_mem_v[l].reshape(bs, mem_len, d), bs, ts, 8)

    tm_e = 256
    h_all = jnp.concatenate([hp, hs], axis=0)
    rt_all = jnp.concatenate([rtp, rts], axis=0)
    eid = rt_all[:, :2].astype(jnp.int32).reshape(-1)
    src, pos, tile_expert, n_tiles = _dispatch_plan(eid, tm_e)
    x_sorted = jnp.take(h_all, jnp.maximum(src, 0) // 2, axis=0)
    y_sorted = _experts(x_sorted, tile_expert, n_tiles, w_exp_gate[l], w_exp_up[l], w_exp_down[l], tm=tm_e)
    pos2 = pos.reshape(tp + ts, 2)
    pick = lambda rows, slot: jnp.take(y_sorted, pos2[rows, slot], axis=0)
    y_prompt = _final(x2p, pick(slice(0, tp), 0), pick(slice(0, tp), 1), rtp, norm_final, tm=512).reshape(bp, sp, d)
    y_sample = _final(x2s, pick(slice(tp, tp + ts), 0), pick(slice(tp, tp + ts), 1), rts, norm_final, tm=ts).reshape(bs, ss, d)

    return (y_prompt, y_sample,
            k.reshape(1, bp, sp, nh, dh), v.reshape(1, bp, sp, nh, dh), plogf.reshape(1, bp, sp, nh),
            mk.reshape(1, bp, mem_len, MEM_N_HEADS, d // MEM_N_HEADS), mv.reshape(1, bp, mem_len, MEM_N_HEADS, d // MEM_N_HEADS),
            pconv[None], pssm[None],
            ks.reshape(1, bs, ss, nh, dh), vs.reshape(1, bs, ss, nh, dh), slogf[:, :ss][None],
            sconv[None], sssm[None])
```

```python
import functools

import numpy as np
import jax
import jax.numpy as jnp
from jax import lax
from jax.experimental import pallas as pl
from jax.experimental.pallas import tpu as pltpu

F32 = jnp.float32
BF16 = jnp.bfloat16
HIGHEST = lax.Precision.HIGHEST

RMS_EPS = 1e-6
LOG2E = 1.4426950408889634
SSD_HEAD_DIM = 64
SSD_N_GROUPS = 4
SSD_D_STATE = 128
SSD_CONV = 4
FOX_N_HEADS = 16
FOX_HEAD_DIM = 64
MEM_N_HEADS = 4
N_EXPERT_GROUPS = 4
EXPERTS_PER_GROUP = 8
N_EXPERTS = N_EXPERT_GROUPS * EXPERTS_PER_GROUP

LANES = 128
VMEM_LIMIT = 56 * 1024 * 1024


def _params(*sem):
    return pltpu.CompilerParams(dimension_semantics=sem, vmem_limit_bytes=VMEM_LIMIT)


def _dot(a, b, **kw):
    return jnp.dot(a, b, preferred_element_type=F32, **kw)


def _dot_nt(a, b, **kw):
    return lax.dot_general(a, b, (((1,), (1,)), ((), ())), preferred_element_type=F32, **kw)


def _dot_tn(a, b, **kw):
    return lax.dot_general(a, b, (((0,), (0,)), ((), ())), preferred_element_type=F32, **kw)


def _softplus(x):
    return jnp.maximum(x, 0.0) + jnp.log1p(jnp.exp(-jnp.abs(x)))


def _sigmoid(x):
    return 1.0 / (1.0 + jnp.exp(-x))


def _silu(x):
    return x * _sigmoid(x)


def _rms(x, g):
    return x * lax.rsqrt(jnp.mean(x * x, axis=-1, keepdims=True) + RMS_EPS) * g


def _split3(f):
    hi = f.astype(BF16).astype(F32)
    r = f - hi
    mid = r.astype(BF16).astype(F32)
    lo = (r - mid).astype(BF16).astype(F32)
    return hi, mid, lo


def _iota2(shape, dim):
    return lax.broadcasted_iota(jnp.int32, shape, dim)


def _rmsnorm_kernel(x_ref, g_ref, o_ref):
    o_ref[...] = _rms(x_ref[...].astype(F32), g_ref[...]).astype(o_ref.dtype)


def _rmsnorm(x, g, tm):
    m, d = x.shape
    return pl.pallas_call(
        _rmsnorm_kernel,
        grid=(m // tm,),
        in_specs=[pl.BlockSpec((tm, d), lambda i: (i, 0)), pl.BlockSpec((1, d), lambda i: (0, 0))],
        out_specs=pl.BlockSpec((tm, d), lambda i: (i, 0)),
        out_shape=jax.ShapeDtypeStruct((m, d), BF16),
        compiler_params=_params("parallel"),
        name="rmsnorm",
    )(x, g.reshape(1, d))


def _mm_kernel(a_ref, w_ref, o_ref, *, scale):
    acc = _dot(a_ref[...], w_ref[0])
    o_ref[...] = (acc * scale if scale != 1.0 else acc).astype(o_ref.dtype)


def _tile_cols(w, tn):
    k, n = w.shape
    return jnp.swapaxes(w.reshape(k, n // tn, tn), 0, 1)


def _matmul(a, w_tiles, out_dtype, *, tm, scale=1.0):
    m, k = a.shape
    nt, _, tn = w_tiles.shape
    return pl.pallas_call(
        functools.partial(_mm_kernel, scale=scale),
        grid=(m // tm, nt),
        in_specs=[pl.BlockSpec((tm, k), lambda i, j: (i, 0)), pl.BlockSpec((1, k, tn), lambda i, j: (j, 0, 0))],
        out_specs=pl.BlockSpec((tm, tn), lambda i, j: (i, j)),
        out_shape=jax.ShapeDtypeStruct((m, nt * tn), out_dtype),
        compiler_params=_params("parallel", "arbitrary"),
        name="matmul",
    )(a, w_tiles)


def _mm_t_kernel(w_ref, a_ref, o_ref):
    o_ref[0] = _dot_nt(w_ref[0], a_ref[...])


def _matmul_t(w_rows, a, *, seqs, tm):
    m, k = a.shape
    nt, tn, _ = w_rows.shape
    per = m // seqs // tm
    return pl.pallas_call(
        _mm_t_kernel,
        grid=(m // tm, nt),
        in_specs=[pl.BlockSpec((1, tn, k), lambda i, j: (j, 0, 0)), pl.BlockSpec((tm, k), lambda i, j: (i, 0))],
        out_specs=pl.BlockSpec((1, tn, tm), lambda i, j: (i // per, j, i % per)),
        out_shape=jax.ShapeDtypeStruct((seqs, nt * tn, m // seqs), F32),
        compiler_params=_params("parallel", "arbitrary"),
        name="matmul_t",
    )(w_rows, a)


def _ssd_kernel(xbc_ref, z_ref, small_ref, dtT_ref, conv0_ref, ssm0_ref,
                convw_ref, convb_ref, dtb_ref, dtbc_ref, alog_ref, alogc_ref, dskip_ref, gnorm_ref, e_ref,
                y_ref, convn_ref, ssm_ref, convbuf, ybuf, *, T, CPS, valid, n_heads, cast):
    step = pl.program_id(1)
    d_inner = n_heads * SSD_HEAD_DIM
    gw = d_inner // SSD_N_GROUPS
    hpg = n_heads // SSD_N_GROUPS
    n = SSD_D_STATE
    k1 = SSD_CONV - 1

    @pl.when(step == 0)
    def _():
        ssm_ref[...] = ssm0_ref[...]
        convbuf[8 - k1:8, :] = conv0_ref[0]

    row_t = _iota2((T, T), 0)
    col_t = _iota2((T, T), 1)
    causal = row_t >= col_t
    lmat = causal.astype(F32)
    umat = (row_t <= col_t).astype(F32)
    lane_lo = _iota2((T, LANES), 1) < SSD_HEAD_DIM
    row_lo = _iota2((LANES, 1), 0) < SSD_HEAD_DIM
    a_row = -jnp.exp(alog_ref[...])
    a_col = -jnp.exp(alogc_ref[...])
    emat = e_ref[...]

    def chunk(c, carry):
        r0 = pl.multiple_of(c * T, T)
        raw = xbc_ref[0, pl.ds(r0, T), :]
        convbuf[8:8 + T, :] = raw
        acc = convb_ref[...] + convbuf[8 - k1:8 - k1 + T, :] * convw_ref[0:1, :]
        for j in range(1, SSD_CONV):
            acc = acc + convbuf[8 - k1 + j:8 - k1 + j + T, :] * convw_ref[j:j + 1, :]
        if valid is None:
            new_conv = raw[T - k1:T, :]
        else:
            new_conv = raw[valid - k1:valid, :]
        convbuf[8 - k1:8, :] = raw[T - k1:T, :]
        convn_ref[0] = new_conv
        xbc = _silu(acc)
        xs = xbc[:, :d_inner]
        bm = xbc[:, d_inner:d_inner + SSD_N_GROUPS * n]
        cm = xbc[:, d_inner + SSD_N_GROUPS * n:]

        dt = _softplus(small_ref[0, pl.ds(r0, T), :][:, :n_heads] + dtb_ref[...])
        dtT = _softplus(dtT_ref[0, c] + dtbc_ref[...])
        if valid is not None:
            dt = jnp.where(_iota2((T, n_heads), 0) < valid, dt, 0.0)
            dtT = jnp.where(_iota2((n_heads, T), 1) < valid, dtT, 0.0)
        a_cs = _dot(lmat, dt * a_row, precision=HIGHEST)
        a_csT = _dot(dtT * a_col, umat, precision=HIGHEST)
        eacs = jnp.exp(a_cs)
        dend = jnp.exp(a_cs[T - 1:T, :] - a_cs)
        expand = lambda x: _dot(jnp.concatenate(_split3(x), axis=1), emat)
        dt_x = expand(dt)
        dend_x = expand(dend)
        eacs_x = expand(eacs)
        xdt = xs * dt_x
        xte = xdt * dend_x

        for g in range(SSD_N_GROUPS):
            bg = cast(bm[:, g * n:(g + 1) * n])
            cg = cast(cm[:, g * n:(g + 1) * n])
            cb = _dot_nt(cg, bg)
            for jp in range(hpg // 2):
                h1 = g * hpg + 2 * jp
                lo = h1 * SSD_HEAD_DIM
                xdt_p = cast(xdt[:, lo:lo + LANES])
                yd = []
                for h in (h1, h1 + 1):
                    seg = a_cs[:, h:h + 1] - a_csT[h:h + 1, :]
                    dec = jnp.exp(jnp.where(causal, seg, -jnp.inf))
                    yd.append(_dot(cast(cb * dec), xdt_p))
                y_diag = jnp.where(lane_lo, yd[0], yd[1])
                st = ssm_ref[0, h1:h1 + 2].reshape(2 * SSD_HEAD_DIM, n)
                y_off = _dot_nt(cg, cast(st)) * eacs_x[:, lo:lo + LANES]
                cs = _dot_tn(cast(xte[:, lo:lo + LANES]), bg)
                dl = jnp.where(row_lo, eacs[T - 1:T, h1:h1 + 1], eacs[T - 1:T, h1 + 1:h1 + 2])
                ssm_ref[0, h1:h1 + 2] = (st * dl + cs).reshape(2, SSD_HEAD_DIM, n)
                ybuf[:, lo:lo + LANES] = (y_diag + y_off) + dskip_ref[:, lo:lo + LANES] * xs[:, lo:lo + LANES]

        zz = z_ref[0, pl.ds(r0, T), :]
        hg = ybuf[...] * _silu(zz)
        for g in range(SSD_N_GROUPS):
            hgg = hg[:, g * gw:(g + 1) * gw]
            y_ref[0, pl.ds(r0, T), g * gw:(g + 1) * gw] = _rms(hgg, gnorm_ref[:, g * gw:(g + 1) * gw]).astype(y_ref.dtype)
        return carry

    lax.fori_loop(0, CPS, chunk, 0)


def _ssd(xbc, z, small, conv0, ssm0, conv_w, conv_b, dt_bias, a_log, d_skip, ssd_norm, *, T, CPS, valid):
    b, l, cdim = xbc.shape
    d_inner = z.shape[-1]
    nh = d_inner // SSD_HEAD_DIM
    rows = T * CPS
    nsteps = l // rows
    assert l % rows == 0 and (valid is None or (nsteps == 1 and CPS == 1 and valid >= SSD_CONV - 1))
    dtT = jnp.swapaxes(small[:, :, :nh].reshape(b, l // T, T, nh), 2, 3)
    emat = jnp.asarray(np.tile(np.repeat(np.eye(nh, dtype=np.float32), SSD_HEAD_DIM, axis=1), (3, 1)))
    dskip_x = jnp.repeat(d_skip.astype(F32), SSD_HEAD_DIM).reshape(1, d_inner)
    aligned = T % 16 == 0
    cast = (lambda v: v.astype(BF16)) if aligned else (lambda v: v)
    full = lambda shape: pl.BlockSpec(shape, lambda i, s: (0,) * len(shape))
    y, convn, ssmn = pl.pallas_call(
        functools.partial(_ssd_kernel, T=T, CPS=CPS, valid=valid, n_heads=nh, cast=cast),
        grid=(b, nsteps),
        in_specs=[
            pl.BlockSpec((1, rows, cdim), lambda i, s: (i, s, 0)),
            pl.BlockSpec((1, rows, d_inner), lambda i, s: (i, s, 0)),
            pl.BlockSpec((1, rows, LANES), lambda i, s: (i, s, 0)),
            pl.BlockSpec((1, CPS, nh, T), lambda i, s: (i, s, 0, 0)),
            pl.BlockSpec((1, SSD_CONV - 1, cdim), lambda i, s: (i, 0, 0)),
            pl.BlockSpec((1, nh, SSD_HEAD_DIM, SSD_D_STATE), lambda i, s: (i, 0, 0, 0)),
            full((SSD_CONV, cdim)), full((1, cdim)), full((1, nh)), full((nh, 1)), full((1, nh)), full((nh, 1)),
            full((1, d_inner)), full((1, d_inner)), full((3 * nh, d_inner)),
        ],
        out_specs=[
            pl.BlockSpec((1, rows, d_inner), lambda i, s: (i, s, 0)),
            pl.BlockSpec((1, SSD_CONV - 1, cdim), lambda i, s: (i, 0, 0)),
            pl.BlockSpec((1, nh, SSD_HEAD_DIM, SSD_D_STATE), lambda i, s: (i, 0, 0, 0)),
        ],
        out_shape=[
            jax.ShapeDtypeStruct((b, l, d_inner), BF16 if aligned else F32),
            jax.ShapeDtypeStruct((b, SSD_CONV - 1, cdim), F32),
            jax.ShapeDtypeStruct((b, nh, SSD_HEAD_DIM, SSD_D_STATE), F32),
        ],
        scratch_shapes=[pltpu.VMEM((8 + T, cdim), F32), pltpu.VMEM((T, d_inner), F32)],
        compiler_params=_params("parallel", "arbitrary"),
        name="ssd",
    )(xbc, z, small, dtT, conv0, ssm0, conv_w, conv_b.reshape(1, cdim), dt_bias.reshape(1, nh), dt_bias.reshape(nh, 1),
      a_log.reshape(1, nh), a_log.reshape(nh, 1), dskip_x, ssd_norm.reshape(1, d_inner), emat)
    return y, convn, ssmn


_AUG = 3


def _fox_gate_kernel(small_ref, q_ref, k_ref, bf_ref, p_ref, c_ref, logf_ref, qa_ref, ka_ref, carry, *, tr, f_off):
    nh = FOX_N_HEADS

    @pl.when(pl.program_id(1) == 0)
    def _():
        carry[...] = jnp.zeros_like(carry)

    logf = -_softplus(-(small_ref[0][:, f_off:f_off + nh] + bf_ref[...]))
    logf_ref[0] = logf
    lmat = (_iota2((tr, tr), 0) >= _iota2((tr, tr), 1)).astype(F32)
    fcum = _dot(jnp.concatenate([lmat] * 3, axis=1), jnp.concatenate(_split3(logf), axis=0)) + carry[...]
    carry[...] = fcum[tr - 1:tr, :]
    x = _dot(jnp.concatenate(_split3(fcum * LOG2E), axis=1), p_ref[...]) + c_ref[...]
    xq = x[:, :nh * LANES]
    xk = x[:, nh * LANES:]
    lane = _iota2((tr, LANES), 1)
    for h in range(nh):
        sel = (lane < FOX_HEAD_DIM) if h % 2 == 0 else (lane >= FOX_HEAD_DIM)
        pair = slice((h // 2) * LANES, (h // 2 + 1) * LANES)
        blk = slice(h * LANES, (h + 1) * LANES)
        qa_ref[0, h] = jnp.where(sel, q_ref[0][:, pair].astype(F32), xq[:, blk]).astype(BF16)
        ka_ref[0, h] = jnp.where(sel, k_ref[0][:, pair], xk[:, blk]).astype(BF16)


def _aug_tables():
    nh = FOX_N_HEADS
    place = np.zeros((_AUG, nh, 2, nh * LANES), np.float32)
    const = np.zeros((1, 2, nh * LANES), np.float32)
    for h in range(nh):
        off = h * LANES + (FOX_HEAD_DIM if h % 2 == 0 else 0)
        for i in range(_AUG):
            place[i, h, 0, off + i] = 1.0
            const[0, 1, off + i] = 1.0
            const[0, 0, off + _AUG + i] = 1.0
            place[i, h, 1, off + _AUG + i] = -1.0
    return jnp.asarray(place.reshape(_AUG * nh, 2 * nh * LANES)), jnp.asarray(const.reshape(1, 2 * nh * LANES))


def _fox_gate(small, q, k, b_fox_f, *, tr, f_off):
    b, l, _ = small.shape
    nh = FOX_N_HEADS
    place, const = _aug_tables()
    full = lambda shape: pl.BlockSpec(shape, lambda i, s: (0,) * len(shape))
    return pl.pallas_call(
        functools.partial(_fox_gate_kernel, tr=tr, f_off=f_off),
        grid=(b, l // tr),
        in_specs=[
            pl.BlockSpec((1, tr, LANES), lambda i, s: (i, s, 0)),
            pl.BlockSpec((1, tr, nh * FOX_HEAD_DIM), lambda i, s: (i, s, 0)),
            pl.BlockSpec((1, tr, nh * FOX_HEAD_DIM), lambda i, s: (i, s, 0)),
            full((1, nh)), full((_AUG * nh, 2 * nh * LANES)), full((1, 2 * nh * LANES)),
        ],
        out_specs=[
            pl.BlockSpec((1, tr, nh), lambda i, s: (i, s, 0)),
            pl.BlockSpec((1, nh, tr, LANES), lambda i, s: (i, 0, s, 0)),
            pl.BlockSpec((1, nh, tr, LANES), lambda i, s: (i, 0, s, 0)),
        ],
        out_shape=[
            jax.ShapeDtypeStruct((b, l, nh), F32),
            jax.ShapeDtypeStruct((b, nh, l, LANES), BF16),
            jax.ShapeDtypeStruct((b, nh, l, LANES), BF16),
        ],
        scratch_shapes=[pltpu.VMEM((1, nh), F32)],
        compiler_params=_params("parallel", "arbitrary"),
        name="fox_gate",
    )(small, q, k, b_fox_f.reshape(1, nh), place, const)


def _fox_prompt_kernel(qi_ref, ki_ref, q_ref, k_ref, v_ref, o_ref, m_scr, acc_scr, *, tq, hps):
    s_idx = pl.program_id(2)
    qi = qi_ref[s_idx]
    ki = ki_ref[s_idx]

    @pl.when(ki == 0)
    def _():
        m_scr[...] = jnp.full_like(m_scr, -jnp.inf)
        acc_scr[...] = jnp.zeros_like(acc_scr)

    def body(masked):
        reps = tq // LANES
        lane_lo = _iota2((tq, LANES), 1) < FOX_HEAD_DIM
        if masked:
            keep = _iota2((tq, tq), 1) <= _iota2((tq, tq), 0)
        for hh in range(hps):
            pair = slice((hh // 2) * LANES, (hh // 2 + 1) * LANES)
            own = lane_lo if hh % 2 == 0 else jnp.logical_not(lane_lo)
            v1 = jnp.where(own, v_ref[0, :, pair], 1.0).astype(BF16)
            s = _dot_nt(q_ref[0, hh], k_ref[0, hh])
            if masked:
                s = jnp.where(keep, s, -jnp.inf)
            m_prev = m_scr[hh]
            m_new = jnp.maximum(m_prev, jnp.max(s, axis=1, keepdims=True))
            p = jnp.exp2(s - jnp.tile(m_new, (1, reps)))
            acc_scr[hh] = jnp.exp2(m_prev - m_new) * acc_scr[hh] + _dot(p.astype(BF16), v1)
            m_scr[hh] = m_new

    pl.when(ki < qi)(lambda: body(False))

    @pl.when(ki == qi)
    def _():
        body(True)
        lane_lo = _iota2((tq, LANES), 1) < FOX_HEAD_DIM
        for pp in range(hps // 2):
            a0, a1 = acc_scr[2 * pp], acc_scr[2 * pp + 1]
            o0 = a0 / pltpu.roll(a0, FOX_HEAD_DIM, axis=1)
            o1 = a1 / pltpu.roll(a1, FOX_HEAD_DIM, axis=1)
            o_ref[0, :, pp * LANES:(pp + 1) * LANES] = jnp.where(lane_lo, o0, o1).astype(o_ref.dtype)


def _fox_prompt(q_aug, k_aug, v, *, tq, hps):
    b, nh, l, _ = q_aug.shape
    nq = l // tq
    vw = hps * FOX_HEAD_DIM
    qi = np.array([i for i in range(nq) for _ in range(i + 1)], np.int32)
    ki = np.array([j for i in range(nq) for j in range(i + 1)], np.int32)
    grid_spec = pltpu.PrefetchScalarGridSpec(
        num_scalar_prefetch=2,
        grid=(b, nh // hps, len(qi)),
        in_specs=[
            pl.BlockSpec((1, hps, tq, LANES), lambda i, p, s, qi, ki: (i, p, qi[s], 0)),
            pl.BlockSpec((1, hps, tq, LANES), lambda i, p, s, qi, ki: (i, p, ki[s], 0)),
            pl.BlockSpec((1, tq, vw), lambda i, p, s, qi, ki: (i, ki[s], p)),
        ],
        out_specs=pl.BlockSpec((1, tq, vw), lambda i, p, s, qi, ki: (i, qi[s], p)),
        scratch_shapes=[pltpu.VMEM((hps, tq, LANES), F32), pltpu.VMEM((hps, tq, LANES), F32)],
    )
    return pl.pallas_call(
        functools.partial(_fox_prompt_kernel, tq=tq, hps=hps),
        grid_spec=grid_spec,
        out_shape=jax.ShapeDtypeStruct((b, l, nh * FOX_HEAD_DIM), BF16),
        compiler_params=_params("parallel", "parallel", "arbitrary"),
        name="fox_prompt",
    )(jnp.asarray(qi), jnp.asarray(ki), q_aug, k_aug, v)


def _gate_scan_kernel(lf_ref, u_ref, o_ref):
    o_ref[...] = _dot(jnp.concatenate(_split3(lf_ref[...]), axis=1), u_ref[...])


def _gate_scan(lf_rows, *, rows):
    n, page = lf_rows.shape
    later = np.triu(np.ones((page, page), np.float32), 1).T
    u3 = jnp.asarray(np.tile(np.concatenate([later, np.ones((page, page), np.float32)], axis=1), (3, 1)))
    return pl.pallas_call(
        _gate_scan_kernel,
        grid=(n // rows,),
        in_specs=[pl.BlockSpec((rows, page), lambda i: (i, 0)), pl.BlockSpec((3 * page, 2 * page), lambda i: (0, 0))],
        out_specs=pl.BlockSpec((rows, 2 * page), lambda i: (i, 0)),
        out_shape=jax.ShapeDtypeStruct((n, 2 * page), F32),
        compiler_params=_params("parallel"),
        name="gate_scan",
    )(lf_rows, u3)


def _fox_sample_kernel(pt_ref, q_ref, kn_ref, vn_ref, small_ref, bf_ref, *rest, PG, page, nq, f_off):
    kt_refs = rest[:PG]
    vt_refs = rest[PG:2 * PG]
    gl_refs = rest[2 * PG:3 * PG]
    o_ref, logf_ref, qx_scr, m_scr, l_scr, acc_scr, carry_scr = rest[3 * PG:]
    nh, dh = FOX_N_HEADS, FOX_HEAD_DIM
    rows = nq * nh
    width = nh * dh
    g = pl.program_id(1)
    own = (_iota2((rows, width), 1) // dh) == (_iota2((rows, width), 0) % nh)

    logf_new = -_softplus(-(small_ref[0][:, f_off:f_off + nh] + bf_ref[...]))
    lf8 = jnp.where(_iota2((8, nh), 0) < nq, logf_new, 0.0)
    tri8 = (_iota2((8, 8), 0) <= _iota2((8, 8), 1)).astype(F32)
    fnew = _dot_tn(tri8, lf8, precision=HIGHEST)
    fnewT = _dot_tn(lf8, tri8, precision=HIGHEST)
    frow = jnp.concatenate([jnp.broadcast_to(fnew[qq:qq + 1, :], (nh, nh)) for qq in range(nq)], axis=0)
    fcol = jnp.sum(jnp.where(_iota2((rows, nh), 1) == _iota2((rows, nh), 0) % nh, frow, 0.0), axis=1, keepdims=True)

    @pl.when(g == 0)
    def _():
        qrows = jnp.concatenate([jnp.broadcast_to(q_ref[0][qq:qq + 1, :], (nh, width)) for qq in range(nq)], axis=0)
        qx_scr[...] = jnp.where(own, qrows, 0.0)
        m_scr[...] = jnp.full_like(m_scr, -jnp.inf)
        l_scr[...] = jnp.zeros_like(l_scr)
        acc_scr[...] = jnp.zeros_like(acc_scr)
        carry_scr[...] = jnp.zeros_like(carry_scr)
        logf_ref[0] = logf_new

    qx = qx_scr[...]

    scores = []
    carry = carry_scr[...]
    for i in range(PG):
        gl = gl_refs[i][...]
        gate = carry + gl[:, :page]
        carry = carry + gl[:, page:]
        scores.append(_dot(qx, kt_refs[i][0]) + fcol + jnp.concatenate([gate] * nq, axis=0))
    carry_scr[...] = carry
    m_prev = m_scr[...]
    m_new = m_prev
    for s in scores:
        m_new = jnp.maximum(m_new, jnp.max(s, axis=1, keepdims=True))
    alpha = jnp.exp(m_prev - m_new)
    l_new = alpha * l_scr[...]
    acc = alpha * acc_scr[...]
    for i, s in enumerate(scores):
        p = jnp.exp(s - m_new)
        l_new = l_new + jnp.sum(p, axis=1, keepdims=True)
        acc = acc + _dot_nt(p, vt_refs[i][0])
    m_scr[...] = m_new
    l_scr[...] = l_new
    acc_scr[...] = acc

    @pl.when(g == pl.num_programs(1) - 1)
    def _():
        s = _dot_nt(qx, kn_ref[0])
        s = s + fcol - jnp.concatenate([fnewT] * nq, axis=0)
        keep = _iota2((rows, 8), 1) <= _iota2((rows, 8), 0) // nh
        s = jnp.where(keep, s, -jnp.inf)
        m_fin = jnp.maximum(m_scr[...], jnp.max(s, axis=1, keepdims=True))
        a_fin = jnp.exp(m_scr[...] - m_fin)
        p = jnp.exp(s - m_fin)
        l_fin = a_fin * l_scr[...] + jnp.sum(p, axis=1, keepdims=True)
        out = jnp.where(own, (a_fin * acc_scr[...] + _dot(p, vn_ref[0])) / l_fin, 0.0)
        pick = (_iota2((8, rows), 1) // nh == _iota2((8, rows), 0)).astype(F32)
        o_ref[0] = _dot(pick, out, precision=HIGHEST).astype(o_ref.dtype)


def _fox_sample(q, k_new, v_new, small, b_fox_f, cache_kt, cache_vt, gates, page_table, *, nq, PG, f_off):
    b, _, width = q.shape
    nh = FOX_N_HEADS
    n_pages = page_table.shape[1]
    page = cache_kt.shape[2]
    assert n_pages % PG == 0
    rows = nq * nh

    def pmap(i):
        return lambda bb, g, pt: (pt[bb, n_pages - 1 - (g * PG + i)], 0, 0)

    def gmap(i):
        return lambda bb, g, pt: (pt[bb, n_pages - 1 - (g * PG + i)], 0)

    fixed = lambda shape: pl.BlockSpec(shape, lambda bb, g, pt: (bb,) + (0,) * (len(shape) - 1))
    in_specs = [fixed((1, 8, width)), fixed((1, 8, width)), fixed((1, 8, width)), fixed((1, 8, LANES)),
                pl.BlockSpec((1, nh), lambda bb, g, pt: (0, 0))]
    in_specs += [pl.BlockSpec((1, width, page), pmap(i)) for i in range(PG)]
    in_specs += [pl.BlockSpec((1, width, page), pmap(i)) for i in range(PG)]
    in_specs += [pl.BlockSpec((nh, 2 * page), gmap(i)) for i in range(PG)]
    grid_spec = pltpu.PrefetchScalarGridSpec(
        num_scalar_prefetch=1,
        grid=(b, n_pages // PG),
        in_specs=in_specs,
        out_specs=[fixed((1, 8, width)), fixed((1, 8, nh))],
        scratch_shapes=[pltpu.VMEM((rows, width), F32), pltpu.VMEM((rows, 1), F32), pltpu.VMEM((rows, 1), F32),
                        pltpu.VMEM((rows, width), F32), pltpu.VMEM((nh, page), F32)],
    )
    return pl.pallas_call(
        functools.partial(_fox_sample_kernel, PG=PG, page=page, nq=nq, f_off=f_off),
        grid_spec=grid_spec,
        out_shape=[jax.ShapeDtypeStruct((b, 8, width), F32), jax.ShapeDtypeStruct((b, 8, nh), F32)],
        compiler_params=_params("parallel", "arbitrary"),
        name="fox_sample",
    )(page_table, q, k_new, v_new, small, b_fox_f.reshape(1, nh),
      *([cache_kt] * PG), *([cache_vt] * PG), *([gates] * PG))


def _mix_kernel(x_ref, ys_ref, yf_ref, gt_ref, bg_ref, wso_ref, wfo_ref, wmx_ref, nm_ref, wq_ref, x1_ref, qm_ref, *, qscale):
    d = x_ref.shape[-1]
    gate = _sigmoid(gt_ref[...] + bg_ref[...])
    merged = gate[:, :d] * _dot(ys_ref[...], wso_ref[...]) + gate[:, d:] * _dot(yf_ref[...], wfo_ref[...])
    x1 = x_ref[...] + _dot(merged.astype(BF16), wmx_ref[...])
    x1_ref[...] = x1
    h = _rms(x1, nm_ref[...]).astype(BF16)
    qm_ref[...] = (_dot(h, wq_ref[...]) * qscale).astype(qm_ref.dtype)


def _mix(x, y_ssd, y_fox, gates, b_gate, w_so, w_fo, w_mx, norm_mem, w_q, *, tm, qscale, q_dtype):
    m, d = x.shape
    row = lambda w: pl.BlockSpec((tm, w), lambda i: (i, 0))
    full = lambda a: pl.BlockSpec(a.shape, lambda i: (0,) * a.ndim)
    bg, nm = b_gate.reshape(1, -1), norm_mem.reshape(1, -1)
    return pl.pallas_call(
        functools.partial(_mix_kernel, qscale=qscale),
        grid=(m // tm,),
        in_specs=[row(d), row(y_ssd.shape[1]), row(d), row(2 * d), full(bg), full(w_so), full(w_fo), full(w_mx), full(nm), full(w_q)],
        out_specs=[row(d), row(d)],
        out_shape=[jax.ShapeDtypeStruct((m, d), F32), jax.ShapeDtypeStruct((m, d), q_dtype)],
        compiler_params=_params("parallel"),
        name="mix",
    )(x, y_ssd, y_fox, gates, bg, w_so, w_fo, w_mx, nm, w_q)


def _mem_attn_kernel(q_ref, k_ref, v_ref, o_ref, *, cast):
    dh = q_ref.shape[-1] // MEM_N_HEADS
    for h in range(MEM_N_HEADS):
        cols = slice(h * dh, (h + 1) * dh)
        s = _dot_nt(cast(q_ref[0][:, cols]), cast(k_ref[0][:, cols]))
        p = jnp.exp(s - jnp.max(s, axis=1, keepdims=True))
        p = p / jnp.sum(p, axis=1, keepdims=True)
        o_ref[0, :, cols] = _dot(cast(p), cast(v_ref[0][:, cols])).astype(o_ref.dtype)


def _mem_attn(q, mem_k, mem_v, *, tl):
    b, l, d = q.shape
    m = mem_k.shape[1]
    cast = (lambda v: v.astype(BF16)) if q.dtype == BF16 else (lambda v: v)
    return pl.pallas_call(
        functools.partial(_mem_attn_kernel, cast=cast),
        grid=(b, l // tl),
        in_specs=[pl.BlockSpec((1, tl, d), lambda i, s: (i, s, 0)),
                  pl.BlockSpec((1, m, d), lambda i, s: (i, 0, 0)),
                  pl.BlockSpec((1, m, d), lambda i, s: (i, 0, 0))],
        out_specs=pl.BlockSpec((1, tl, d), lambda i, s: (i, s, 0)),
        out_shape=jax.ShapeDtypeStruct((b, l, d), q.dtype),
        compiler_params=_params("parallel", "parallel"),
        name="mem_attn",
    )(q, mem_k, mem_v)


def _route_kernel(x1_ref, att_ref, wo_ref, nf_ref, wr_ref, br_ref, x2_ref, h_ref, rt_ref):
    x2 = x1_ref[...] + _dot(att_ref[...], wo_ref[...])
    x2_ref[...] = x2
    h = _rms(x2, nf_ref[...])
    h_ref[...] = h.astype(h_ref.dtype)
    h_hi = h.astype(BF16)
    h_lo = (h - h_hi.astype(F32)).astype(BF16)
    logits = _dot(jnp.concatenate([h_hi, h_hi, h_lo], axis=1), wr_ref[...]) + br_ref[...]
    lane = _iota2(logits.shape, 1).astype(F32)
    first = lambda mask: jnp.min(jnp.where(mask, lane, float(LANES)), axis=1, keepdims=True)
    gl = jnp.where(lane < N_EXPERT_GROUPS, logits, -jnp.inf)
    gmax = jnp.max(gl, axis=1, keepdims=True)
    g_idx = first(gl == gmax)
    g_w = 1.0 / jnp.sum(jnp.exp(gl - gmax), axis=1, keepdims=True)
    e_lo = N_EXPERT_GROUPS + g_idx * EXPERTS_PER_GROUP
    el = jnp.where((lane >= e_lo) & (lane < e_lo + EXPERTS_PER_GROUP), logits, -jnp.inf)
    v1 = jnp.max(el, axis=1, keepdims=True)
    i1 = first(el == v1)
    el2 = jnp.where(lane == i1, -jnp.inf, el)
    v2 = jnp.max(el2, axis=1, keepdims=True)
    i2 = first(el2 == v2)
    e21 = jnp.exp(v2 - v1)
    w1 = g_w / (1.0 + e21)
    w2 = g_w * e21 / (1.0 + e21)
    rt_ref[...] = jnp.where(lane == 0, i1 - N_EXPERT_GROUPS, jnp.where(lane == 1, i2 - N_EXPERT_GROUPS,
                            jnp.where(lane == 2, w1, jnp.where(lane == 3, w2, 0.0))))


def _route(x1, att, w_o, norm_ffn, w_router, b_router, *, tm):
    m, d = x1.shape
    row = lambda w: pl.BlockSpec((tm, w), lambda i: (i, 0))
    full = lambda a: pl.BlockSpec(a.shape, lambda i: (0,) * a.ndim)
    nf = norm_ffn.reshape(1, d)
    return pl.pallas_call(
        _route_kernel,
        grid=(m // tm,),
        in_specs=[row(d), row(d), full(w_o), full(nf), full(w_router), full(b_router)],
        out_specs=[row(d), row(d), row(LANES)],
        out_shape=[jax.ShapeDtypeStruct((m, d), F32), jax.ShapeDtypeStruct((m, d), F32), jax.ShapeDtypeStruct((m, LANES), F32)],
        compiler_params=_params("parallel"),
        name="route",
    )(x1, att, w_o, nf, w_router, b_router)


def _expert_kernel(te_ref, nt_ref, x_ref, wg_ref, wu_ref, wd_ref, o_ref):
    @pl.when(pl.program_id(0) < nt_ref[0])
    def _():
        x = x_ref[...].astype(BF16)
        hid = _silu(_dot(x, wg_ref[0].astype(BF16))) * _dot(x, wu_ref[0].astype(BF16))
        o_ref[...] = _dot(hid.astype(BF16), wd_ref[0].astype(BF16)).astype(o_ref.dtype)

    @pl.when(pl.program_id(0) >= nt_ref[0])
    def _():
        o_ref[...] = jnp.zeros_like(o_ref)


def _experts(x_sorted, tile_expert, n_tiles, w_gate, w_up, w_down, *, tm):
    p, d = x_sorted.shape
    ne, _, ff = w_gate.shape
    grid_spec = pltpu.PrefetchScalarGridSpec(
        num_scalar_prefetch=2,
        grid=(p // tm,),
        in_specs=[pl.BlockSpec((tm, d), lambda i, te, nt: (i, 0)),
                  pl.BlockSpec((1, d, ff), lambda i, te, nt: (te[i], 0, 0)),
                  pl.BlockSpec((1, d, ff), lambda i, te, nt: (te[i], 0, 0)),
                  pl.BlockSpec((1, ff, d), lambda i, te, nt: (te[i], 0, 0))],
        out_specs=pl.BlockSpec((tm, d), lambda i, te, nt: (i, 0)),
    )
    return pl.pallas_call(
        _expert_kernel,
        grid_spec=grid_spec,
        out_shape=jax.ShapeDtypeStruct((p, d), F32),
        compiler_params=_params("arbitrary"),
        name="experts",
    )(tile_expert, n_tiles, x_sorted, w_gate, w_up, w_down)


def _final_kernel(x2_ref, ya_ref, yb_ref, rt_ref, g_ref, o_ref):
    rt = rt_ref[...]
    moe = rt[:, 2:3] * ya_ref[...] + rt[:, 3:4] * yb_ref[...]
    o_ref[...] = _rms(x2_ref[...] + moe, g_ref[...])


def _final(x2, ya, yb, rt, norm_final, *, tm):
    m, d = x2.shape
    row = lambda w: pl.BlockSpec((tm, w), lambda i: (i, 0))
    return pl.pallas_call(
        _final_kernel,
        grid=(m // tm,),
        in_specs=[row(d), row(d), row(d), row(LANES), pl.BlockSpec((1, d), lambda i: (0, 0))],
        out_specs=row(d),
        out_shape=jax.ShapeDtypeStruct((m, d), F32),
        compiler_params=_params("parallel"),
        name="final",
    )(x2, ya, yb, rt, norm_final.reshape(1, d))


def _dispatch_plan(eid, tm):
    n = eid.shape[0]
    p = (n + N_EXPERTS * (tm - 1) + tm - 1) // tm * tm
    onehot = (eid[:, None] == jnp.arange(N_EXPERTS, dtype=jnp.int32)[None, :]).astype(jnp.int32)
    csum = jnp.cumsum(onehot, axis=0)
    counts = csum[-1]
    rank = jnp.sum(csum * onehot, axis=1) - 1
    padded = (counts + tm - 1) // tm * tm
    pend = jnp.cumsum(padded)
    pstart = pend - padded
    pos = jnp.sum(pstart[None, :] * onehot, axis=1) + rank
    src = jnp.full((p,), -1, jnp.int32).at[pos].set(jnp.arange(n, dtype=jnp.int32), unique_indices=True)
    n_tiles = (pend[-1] // tm).astype(jnp.int32)
    tile_first = jnp.arange(p // tm, dtype=jnp.int32) * tm
    tile_expert = jnp.sum((tile_first[:, None] >= pend[None, :]).astype(jnp.int32), axis=1)
    last_used = jnp.sum((jnp.maximum(pend[-1] - tm, 0) >= pend).astype(jnp.int32))
    tile_expert = jnp.where(tile_first < pend[-1], tile_expert, last_used).astype(jnp.int32)
    return src, pos, tile_expert, n_tiles.reshape(1)


def _bf(w):
    return w.astype(BF16)


def kernel(x_prompt, x_sample, mem_prompt, cache_fox_k, cache_fox_v, cache_fox_logf, page_table, cache_mem_k, cache_mem_v, state_conv, state_ssm, norm_mix, w_in, conv_w, conv_b, dt_bias, a_log, d_skip, ssd_norm, w_ssd_out, b_fox_f, w_fox_out, b_gate, w_mix_out, norm_mem, norm_mem_kv, w_mem_q, w_mem_k, w_mem_v, w_mem_o, norm_ffn, w_router_group, b_router_group, w_router_expert, b_router_expert, w_exp_gate, w_exp_up, w_exp_down, norm_final):
    depth = w_in.shape[0]
    assert depth == 1, "single-layer step"
    bp, sp, d = x_prompt.shape
    bs, ss, _ = x_sample.shape
    tp, ts = bp * sp, bs * ss
    d_inner = ssd_norm.shape[-1]
    cdim = conv_w.shape[-1]
    nh_ssd = dt_bias.shape[-1]
    nh, dh = FOX_N_HEADS, FOX_HEAD_DIM
    fw = nh * dh
    mem_len = mem_prompt.shape[1]
    l = 0

    o_z, o_x, o_dt, o_q, o_k, o_v, o_f, o_g = np.cumsum([0, d_inner, cdim, nh_ssd, fw, fw, fw, nh]).tolist()
    wi = w_in[l]
    tn = 1024
    seg = lambda a, b: _tile_cols(_bf(wi[:, a:b]), tn)
    w_z, w_x, w_qf, w_kf, w_vf, w_g = seg(o_z, o_x), seg(o_x, o_dt), seg(o_q, o_k), seg(o_k, o_v), seg(o_v, o_f), seg(o_g, o_g + 2 * d)
    n_small = nh_ssd + nh
    w_small = _bf(jnp.pad(jnp.concatenate([wi[:, o_dt:o_q], wi[:, o_f:o_g]], axis=1), ((0, 0), (0, LANES - n_small))))[None]
    f_off = nh_ssd
    w_so, w_fo, w_mx, w_q, w_o = _bf(w_ssd_out[l]), _bf(w_fox_out[l]), _bf(w_mix_out[l]), _bf(w_mem_q[l]), _bf(w_mem_o[l])
    n_r = N_EXPERT_GROUPS + N_EXPERTS
    w_router = jnp.pad(jnp.concatenate([w_router_group[l], w_router_expert[l]], axis=1), ((0, 0), (0, LANES - n_r)))
    wr_hi = _bf(w_router)
    wr_lo = _bf(w_router - wr_hi.astype(F32))
    w_router = jnp.concatenate([wr_hi, wr_lo, wr_hi], axis=0)
    b_router = jnp.pad(jnp.concatenate([b_router_group[l], b_router_expert[l]]), (0, LANES - n_r)).reshape(1, LANES)
    mem_scale = (d // MEM_N_HEADS) ** -0.5

    def in_proj(x2d, tm, q_dtype, q_scale):
        h = _rmsnorm(x2d, norm_mix[l], tm)
        mm = lambda w, dt=F32, sc=1.0: _matmul(h, w, dt, tm=tm, scale=sc)
        return (mm(w_z), mm(w_x), mm(w_qf, q_dtype, q_scale), mm(w_kf), mm(w_vf), mm(w_g), mm(w_small)), h

    def post(x2d, y_ssd, y_fox, gates, mem_k, mem_v, nb, tm, tl):
        per = x2d.shape[0] // nb
        q_dtype = BF16 if per % 16 == 0 else F32
        x1, qm = _mix(x2d, y_ssd, y_fox, gates, b_gate[l], w_so, w_fo, w_mx, norm_mem[l], w_q, tm=tm, qscale=mem_scale, q_dtype=q_dtype)
        qm = qm.reshape(nb, per, d)
        if per % 8:
            qm = jnp.pad(qm, ((0, 0), (0, 8 - per % 8), (0, 0)))
        att = _mem_attn(qm, mem_k, mem_v, tl=tl)[:, :per].reshape(-1, d).astype(BF16)
        return _route(x1, att, w_o, norm_ffn[l], w_router, b_router, tm=tm)

    xp = x_prompt.reshape(tp, d)
    (z, xbc, q, k, v, gates, small), h_in = in_proj(xp, 1024, BF16, dh ** -0.5 * LOG2E)
    k_out = _matmul_t(_bf(wi[:, o_k:o_v]).T[None], h_in, seqs=bp, tm=1024).reshape(bp, nh, dh, sp).transpose(0, 3, 1, 2)
    v_out = _matmul_t(_bf(wi[:, o_v:o_f]).T[None], h_in, seqs=bp, tm=1024).reshape(bp, nh, dh, sp).transpose(0, 3, 1, 2)
    conv0 = jnp.zeros((bp, SSD_CONV - 1, cdim), F32)
    ssm0 = jnp.zeros((bp, nh_ssd, SSD_HEAD_DIM, SSD_D_STATE), F32)
    y_ssd, pconv, pssm = _ssd(xbc.reshape(bp, sp, cdim), z.reshape(bp, sp, d_inner), small.reshape(bp, sp, LANES), conv0, ssm0,
                              conv_w[l], conv_b[l], dt_bias[l], a_log[l], d_skip[l], ssd_norm[l], T=128, CPS=4, valid=None)
    plogf, q_aug, k_aug = _fox_gate(small.reshape(bp, sp, LANES), q.reshape(bp, sp, fw), k.reshape(bp, sp, fw), b_fox_f[l], tr=512, f_off=f_off)
    y_fox = _fox_prompt(q_aug, k_aug, v.reshape(bp, sp, fw), tq=1024, hps=4)
    mem_h = _rmsnorm(mem_prompt.reshape(bp * mem_len, d), norm_mem_kv[l], 256)
    mk = _matmul(mem_h, _tile_cols(_bf(w_mem_k[l]), 512), F32, tm=256)
    mv = _matmul(mem_h, _tile_cols(_bf(w_mem_v[l]), 512), F32, tm=256)
    x2p, hp, rtp = post(xp, y_ssd.reshape(tp, d_inner), y_fox.reshape(tp, fw), gates, mk.reshape(bp, mem_len, d), mv.reshape(bp, mem_len, d), bp, 256, 512)

    xs = x_sample.reshape(ts, d)
    (zs, xbcs, qs, ks, vs, gates_s, small_s), _ = in_proj(xs, ts, F32, dh ** -0.5)
    pad8 = lambda a: jnp.pad(a.reshape(bs, ss, -1), ((0, 0), (0, 8 - ss), (0, 0)))
    ys_ssd, sconv, sssm = _ssd(pad8(xbcs), pad8(zs), pad8(small_s), state_conv[l], state_ssm[l],
                               conv_w[l], conv_b[l], dt_bias[l], a_log[l], d_skip[l], ssd_norm[l], T=8, CPS=1, valid=ss)
    pool, page = cache_fox_k.shape[1], cache_fox_k.shape[2]
    kt = jnp.transpose(cache_fox_k[l], (0, 2, 3, 1)).reshape(pool, fw, page)
    vt = jnp.transpose(cache_fox_v[l], (0, 2, 3, 1)).reshape(pool, fw, page)
    gates_past = _gate_scan(jnp.transpose(cache_fox_logf[l], (0, 2, 1)).reshape(pool * nh, page), rows=2048)
    ys_fox, slogf = _fox_sample(pad8(qs), pad8(ks), pad8(vs), pad8(small_s), b_fox_f[l], kt, vt, gates_past, page_table,
                                nq=ss, PG=16, f_off=f_off)
    x2s, hs, rts = post(xs, ys_ssd[:, :ss].reshape(ts, d_inner).astype(BF16), ys_fox[:, :ss].reshape(ts, fw).astype(BF16), gates_s,
                        cache_mem_k[l].reshape(bs, mem_len, d), cache_mem_v[l].reshape(bs, mem_len, d), bs, ts, 8)

    tm_e = 256
    h_all = jnp.concatenate([hp, hs], axis=0)
    rt_all = jnp.concatenate([rtp, rts], axis=0)
    eid = rt_all[:, :2].astype(jnp.int32).reshape(-1)
    src, pos, tile_expert, n_tiles = _dispatch_plan(eid, tm_e)
    x_sorted = jnp.take(h_all, jnp.maximum(src, 0) // 2, axis=0)
    y_sorted = _experts(x_sorted, tile_expert, n_tiles, w_exp_gate[l], w_exp_up[l], w_exp_down[l], tm=tm_e)
    pos2 = pos.reshape(tp + ts, 2)
    pick = lambda rows, slot: jnp.take(y_sorted, pos2[rows, slot], axis=0)
    y_prompt = _final(x2p, pick(slice(0, tp), 0), pick(slice(0, tp), 1), rtp, norm_final, tm=512).reshape(bp, sp, d)
    y_sample = _final(x2s, pick(slice(tp, tp + ts), 0), pick(slice(tp, tp + ts), 1), rts, norm_final, tm=ts).reshape(bs, ss, d)

    return (y_prompt, y_sample,
            k_out[None], v_out[None], plogf.reshape(1, bp, sp, nh),
            mk.reshape(1, bp, mem_len, MEM_N_HEADS, d // MEM_N_HEADS), mv.reshape(1, bp, mem_len, MEM_N_HEADS, d // MEM_N_HEADS),
            pconv[None], pssm[None],
            ks.reshape(1, bs, ss, nh, dh), vs.reshape(1, bs, ss, nh, dh), slogf[:, :ss][None],
            sconv[None], sssm[None])
```

```python
import functools

import numpy as np
import jax
import jax.numpy as jnp
from jax import lax
from jax.experimental import pallas as pl
from jax.experimental.pallas import tpu as pltpu

F32 = jnp.float32
BF16 = jnp.bfloat16
HIGHEST = lax.Precision.HIGHEST

RMS_EPS = 1e-6
LOG2E = 1.4426950408889634
SSD_HEAD_DIM = 64
SSD_N_GROUPS = 4
SSD_D_STATE = 128
SSD_CONV = 4
FOX_N_HEADS = 16
FOX_HEAD_DIM = 64
MEM_N_HEADS = 4
N_EXPERT_GROUPS = 4
EXPERTS_PER_GROUP = 8
N_EXPERTS = N_EXPERT_GROUPS * EXPERTS_PER_GROUP

LANES = 128
VMEM_LIMIT = 56 * 1024 * 1024


def _params(*sem):
    return pltpu.CompilerParams(dimension_semantics=sem, vmem_limit_bytes=VMEM_LIMIT)


def _dot(a, b, **kw):
    return jnp.dot(a, b, preferred_element_type=F32, **kw)


def _dot_nt(a, b, **kw):
    return lax.dot_general(a, b, (((1,), (1,)), ((), ())), preferred_element_type=F32, **kw)


def _dot_tn(a, b, **kw):
    return lax.dot_general(a, b, (((0,), (0,)), ((), ())), preferred_element_type=F32, **kw)


def _softplus(x):
    return jnp.maximum(x, 0.0) + jnp.log1p(jnp.exp(-jnp.abs(x)))


def _sigmoid(x):
    return 1.0 / (1.0 + jnp.exp(-x))


def _silu(x):
    return x * _sigmoid(x)


def _rms(x, g):
    return x * lax.rsqrt(jnp.mean(x * x, axis=-1, keepdims=True) + RMS_EPS) * g


def _split3(f):
    hi = f.astype(BF16).astype(F32)
    r = f - hi
    mid = r.astype(BF16).astype(F32)
    lo = (r - mid).astype(BF16).astype(F32)
    return hi, mid, lo


def _iota2(shape, dim):
    return lax.broadcasted_iota(jnp.int32, shape, dim)


def _rmsnorm_kernel(x_ref, g_ref, o_ref):
    o_ref[...] = _rms(x_ref[...].astype(F32), g_ref[...]).astype(o_ref.dtype)


def _rmsnorm(x, g, tm):
    m, d = x.shape
    return pl.pallas_call(
        _rmsnorm_kernel,
        grid=(m // tm,),
        in_specs=[pl.BlockSpec((tm, d), lambda i: (i, 0)), pl.BlockSpec((1, d), lambda i: (0, 0))],
        out_specs=pl.BlockSpec((tm, d), lambda i: (i, 0)),
        out_shape=jax.ShapeDtypeStruct((m, d), BF16),
        compiler_params=_params("parallel"),
        name="rmsnorm",
    )(x, g.reshape(1, d))


def _mm_kernel(a_ref, w_ref, o_ref, *, scale):
    acc = _dot(a_ref[...], w_ref[0])
    o_ref[...] = (acc * scale if scale != 1.0 else acc).astype(o_ref.dtype)


def _tile_cols(w, tn):
    k, n = w.shape
    return jnp.swapaxes(w.reshape(k, n // tn, tn), 0, 1)


def _matmul(a, w_tiles, out_dtype, *, tm, scale=1.0):
    m, k = a.shape
    nt, _, tn = w_tiles.shape
    return pl.pallas_call(
        functools.partial(_mm_kernel, scale=scale),
        grid=(m // tm, nt),
        in_specs=[pl.BlockSpec((tm, k), lambda i, j: (i, 0)), pl.BlockSpec((1, k, tn), lambda i, j: (j, 0, 0))],
        out_specs=pl.BlockSpec((tm, tn), lambda i, j: (i, j)),
        out_shape=jax.ShapeDtypeStruct((m, nt * tn), out_dtype),
        compiler_params=_params("parallel", "arbitrary"),
        name="matmul",
    )(a, w_tiles)


def _mm_t_kernel(w_ref, a_ref, o_ref):
    o_ref[0] = _dot_nt(w_ref[0], a_ref[...])


def _matmul_t(w_rows, a, *, seqs, tm):
    m, k = a.shape
    nt, tn, _ = w_rows.shape
    per = m // seqs // tm
    return pl.pallas_call(
        _mm_t_kernel,
        grid=(m // tm, nt),
        in_specs=[pl.BlockSpec((1, tn, k), lambda i, j: (j, 0, 0)), pl.BlockSpec((tm, k), lambda i, j: (i, 0))],
        out_specs=pl.BlockSpec((1, tn, tm), lambda i, j: (i // per, j, i % per)),
        out_shape=jax.ShapeDtypeStruct((seqs, nt * tn, m // seqs), F32),
        compiler_params=_params("parallel", "arbitrary"),
        name="matmul_t",
    )(w_rows, a)


def _ssd_kernel(xbc_ref, z_ref, small_ref, dtT_ref, conv0_ref, ssm0_ref,
                convw_ref, convb_ref, dtb_ref, dtbc_ref, alog_ref, alogc_ref, dskip_ref, gnorm_ref, e_ref,
                y_ref, convn_ref, ssm_ref, convbuf, ybuf, *, T, CPS, valid, n_heads, cast):
    step = pl.program_id(1)
    d_inner = n_heads * SSD_HEAD_DIM
    gw = d_inner // SSD_N_GROUPS
    hpg = n_heads // SSD_N_GROUPS
    n = SSD_D_STATE
    k1 = SSD_CONV - 1

    @pl.when(step == 0)
    def _():
        ssm_ref[...] = ssm0_ref[...]
        convbuf[8 - k1:8, :] = conv0_ref[0]

    row_t = _iota2((T, T), 0)
    col_t = _iota2((T, T), 1)
    causal = row_t >= col_t
    lmat = causal.astype(F32)
    umat = (row_t <= col_t).astype(F32)
    lane_lo = _iota2((T, LANES), 1) < SSD_HEAD_DIM
    row_lo = _iota2((LANES, 1), 0) < SSD_HEAD_DIM
    a_row = -jnp.exp(alog_ref[...])
    a_col = -jnp.exp(alogc_ref[...])
    emat = e_ref[...]

    def chunk(c, carry):
        r0 = pl.multiple_of(c * T, T)
        raw = xbc_ref[0, pl.ds(r0, T), :]
        convbuf[8:8 + T, :] = raw
        acc = convb_ref[...] + convbuf[8 - k1:8 - k1 + T, :] * convw_ref[0:1, :]
        for j in range(1, SSD_CONV):
            acc = acc + convbuf[8 - k1 + j:8 - k1 + j + T, :] * convw_ref[j:j + 1, :]
        if valid is None:
            new_conv = raw[T - k1:T, :]
        else:
            new_conv = raw[valid - k1:valid, :]
        convbuf[8 - k1:8, :] = raw[T - k1:T, :]
        convn_ref[0] = new_conv
        xbc = _silu(acc)
        xs = xbc[:, :d_inner]
        bm = xbc[:, d_inner:d_inner + SSD_N_GROUPS * n]
        cm = xbc[:, d_inner + SSD_N_GROUPS * n:]

        dt = _softplus(small_ref[0, pl.ds(r0, T), :][:, :n_heads] + dtb_ref[...])
        dtT = _softplus(dtT_ref[0, c] + dtbc_ref[...])
        if valid is not None:
            dt = jnp.where(_iota2((T, n_heads), 0) < valid, dt, 0.0)
            dtT = jnp.where(_iota2((n_heads, T), 1) < valid, dtT, 0.0)
        a_cs = _dot(lmat, dt * a_row, precision=HIGHEST)
        a_csT = _dot(dtT * a_col, umat, precision=HIGHEST)
        eacs = jnp.exp(a_cs)
        dend = jnp.exp(a_cs[T - 1:T, :] - a_cs)
        expand = lambda x: _dot(jnp.concatenate(_split3(x), axis=1), emat)
        dt_x = expand(dt)
        dend_x = expand(dend)
        eacs_x = expand(eacs)
        xdt = xs * dt_x
        xte = xdt * dend_x

        for g in range(SSD_N_GROUPS):
            bg = cast(bm[:, g * n:(g + 1) * n])
            cg = cast(cm[:, g * n:(g + 1) * n])
            cb = _dot_nt(cg, bg)
            for jp in range(hpg // 2):
                h1 = g * hpg + 2 * jp
                lo = h1 * SSD_HEAD_DIM
                xdt_p = cast(xdt[:, lo:lo + LANES])
                yd = []
                for h in (h1, h1 + 1):
                    seg = a_cs[:, h:h + 1] - a_csT[h:h + 1, :]
                    dec = jnp.exp(jnp.where(causal, seg, -jnp.inf))
                    yd.append(_dot(cast(cb * dec), xdt_p))
                y_diag = jnp.where(lane_lo, yd[0], yd[1])
                st = ssm_ref[0, h1:h1 + 2].reshape(2 * SSD_HEAD_DIM, n)
                y_off = _dot_nt(cg, cast(st)) * eacs_x[:, lo:lo + LANES]
                cs = _dot_tn(cast(xte[:, lo:lo + LANES]), bg)
                dl = jnp.where(row_lo, eacs[T - 1:T, h1:h1 + 1], eacs[T - 1:T, h1 + 1:h1 + 2])
                ssm_ref[0, h1:h1 + 2] = (st * dl + cs).reshape(2, SSD_HEAD_DIM, n)
                ybuf[:, lo:lo + LANES] = (y_diag + y_off) + dskip_ref[:, lo:lo + LANES] * xs[:, lo:lo + LANES]

        zz = z_ref[0, pl.ds(r0, T), :]
        hg = ybuf[...] * _silu(zz)
        for g in range(SSD_N_GROUPS):
            hgg = hg[:, g * gw:(g + 1) * gw]
            y_ref[0, pl.ds(r0, T), g * gw:(g + 1) * gw] = _rms(hgg, gnorm_ref[:, g * gw:(g + 1) * gw]).astype(y_ref.dtype)
        return carry

    lax.fori_loop(0, CPS, chunk, 0)


def _ssd(xbc, z, small, conv0, ssm0, conv_w, conv_b, dt_bias, a_log, d_skip, ssd_norm, *, T, CPS, valid):
    b, l, cdim = xbc.shape
    d_inner = z.shape[-1]
    nh = d_inner // SSD_HEAD_DIM
    rows = T * CPS
    nsteps = l // rows
    assert l % rows == 0 and (valid is None or (nsteps == 1 and CPS == 1 and valid >= SSD_CONV - 1))
    dtT = jnp.swapaxes(small[:, :, :nh].reshape(b, l // T, T, nh), 2, 3)
    emat = jnp.asarray(np.tile(np.repeat(np.eye(nh, dtype=np.float32), SSD_HEAD_DIM, axis=1), (3, 1)))
    dskip_x = jnp.repeat(d_skip.astype(F32), SSD_HEAD_DIM).reshape(1, d_inner)
    aligned = T % 16 == 0
    cast = (lambda v: v.astype(BF16)) if aligned else (lambda v: v)
    full = lambda shape: pl.BlockSpec(shape, lambda i, s: (0,) * len(shape))
    y, convn, ssmn = pl.pallas_call(
        functools.partial(_ssd_kernel, T=T, CPS=CPS, valid=valid, n_heads=nh, cast=cast),
        grid=(b, nsteps),
        in_specs=[
            pl.BlockSpec((1, rows, cdim), lambda i, s: (i, s, 0)),
            pl.BlockSpec((1, rows, d_inner), lambda i, s: (i, s, 0)),
            pl.BlockSpec((1, rows, LANES), lambda i, s: (i, s, 0)),
            pl.BlockSpec((1, CPS, nh, T), lambda i, s: (i, s, 0, 0)),
            pl.BlockSpec((1, SSD_CONV - 1, cdim), lambda i, s: (i, 0, 0)),
            pl.BlockSpec((1, nh, SSD_HEAD_DIM, SSD_D_STATE), lambda i, s: (i, 0, 0, 0)),
            full((SSD_CONV, cdim)), full((1, cdim)), full((1, nh)), full((nh, 1)), full((1, nh)), full((nh, 1)),
            full((1, d_inner)), full((1, d_inner)), full((3 * nh, d_inner)),
        ],
        out_specs=[
            pl.BlockSpec((1, rows, d_inner), lambda i, s: (i, s, 0)),
            pl.BlockSpec((1, SSD_CONV - 1, cdim), lambda i, s: (i, 0, 0)),
            pl.BlockSpec((1, nh, SSD_HEAD_DIM, SSD_D_STATE), lambda i, s: (i, 0, 0, 0)),
        ],
        out_shape=[
            jax.ShapeDtypeStruct((b, l, d_inner), BF16 if aligned else F32),
            jax.ShapeDtypeStruct((b, SSD_CONV - 1, cdim), F32),
            jax.ShapeDtypeStruct((b, nh, SSD_HEAD_DIM, SSD_D_STATE), F32),
        ],
        scratch_shapes=[pltpu.VMEM((8 + T, cdim), F32), pltpu.VMEM((T, d_inner), F32)],
        compiler_params=_params("parallel", "arbitrary"),
        name="ssd",
    )(xbc, z, small, dtT, conv0, ssm0, conv_w, conv_b.reshape(1, cdim), dt_bias.reshape(1, nh), dt_bias.reshape(nh, 1),
      a_log.reshape(1, nh), a_log.reshape(nh, 1), dskip_x, ssd_norm.reshape(1, d_inner), emat)
    return y, convn, ssmn


_AUG = 3


def _fox_gate_kernel(small_ref, q_ref, k_ref, bf_ref, p_ref, c_ref, logf_ref, qa_ref, ka_ref, carry, *, tr, f_off):
    nh = FOX_N_HEADS

    @pl.when(pl.program_id(1) == 0)
    def _():
        carry[...] = jnp.zeros_like(carry)

    logf = -_softplus(-(small_ref[0][:, f_off:f_off + nh] + bf_ref[...]))
    logf_ref[0] = logf
    lmat = (_iota2((tr, tr), 0) >= _iota2((tr, tr), 1)).astype(F32)
    fcum = _dot(jnp.concatenate([lmat] * 3, axis=1), jnp.concatenate(_split3(logf), axis=0)) + carry[...]
    carry[...] = fcum[tr - 1:tr, :]
    x = _dot(jnp.concatenate(_split3(fcum * LOG2E), axis=1), p_ref[...]) + c_ref[...]
    xq = x[:, :nh * LANES]
    xk = x[:, nh * LANES:]
    lane = _iota2((tr, LANES), 1)
    for h in range(nh):
        sel = (lane < FOX_HEAD_DIM) if h % 2 == 0 else (lane >= FOX_HEAD_DIM)
        pair = slice((h // 2) * LANES, (h // 2 + 1) * LANES)
        blk = slice(h * LANES, (h + 1) * LANES)
        qa_ref[0, h] = jnp.where(sel, q_ref[0][:, pair].astype(F32), xq[:, blk]).astype(BF16)
        ka_ref[0, h] = jnp.where(sel, k_ref[0][:, pair], xk[:, blk]).astype(BF16)


def _aug_tables():
    nh = FOX_N_HEADS
    place = np.zeros((_AUG, nh, 2, nh * LANES), np.float32)
    const = np.zeros((1, 2, nh * LANES), np.float32)
    for h in range(nh):
        off = h * LANES + (FOX_HEAD_DIM if h % 2 == 0 else 0)
        for i in range(_AUG):
            place[i, h, 0, off + i] = 1.0
            const[0, 1, off + i] = 1.0
            const[0, 0, off + _AUG + i] = 1.0
            place[i, h, 1, off + _AUG + i] = -1.0
    return jnp.asarray(place.reshape(_AUG * nh, 2 * nh * LANES)), jnp.asarray(const.reshape(1, 2 * nh * LANES))


def _fox_gate(small, q, k, b_fox_f, *, tr, f_off):
    b, l, _ = small.shape
    nh = FOX_N_HEADS
    place, const = _aug_tables()
    full = lambda shape: pl.BlockSpec(shape, lambda i, s: (0,) * len(shape))
    return pl.pallas_call(
        functools.partial(_fox_gate_kernel, tr=tr, f_off=f_off),
        grid=(b, l // tr),
        in_specs=[
            pl.BlockSpec((1, tr, LANES), lambda i, s: (i, s, 0)),
            pl.BlockSpec((1, tr, nh * FOX_HEAD_DIM), lambda i, s: (i, s, 0)),
            pl.BlockSpec((1, tr, nh * FOX_HEAD_DIM), lambda i, s: (i, s, 0)),
            full((1, nh)), full((_AUG * nh, 2 * nh * LANES)), full((1, 2 * nh * LANES)),
        ],
        out_specs=[
            pl.BlockSpec((1, tr, nh), lambda i, s: (i, s, 0)),
            pl.BlockSpec((1, nh, tr, LANES), lambda i, s: (i, 0, s, 0)),
            pl.BlockSpec((1, nh, tr, LANES), lambda i, s: (i, 0, s, 0)),
        ],
        out_shape=[
            jax.ShapeDtypeStruct((b, l, nh), F32),
            jax.ShapeDtypeStruct((b, nh, l, LANES), BF16),
            jax.ShapeDtypeStruct((b, nh, l, LANES), BF16),
        ],
        scratch_shapes=[pltpu.VMEM((1, nh), F32)],
        compiler_params=_params("parallel", "arbitrary"),
        name="fox_gate",
    )(small, q, k, b_fox_f.reshape(1, nh), place, const)


def _fox_prompt_kernel(qi_ref, ki_ref, q_ref, k_ref, v_ref, o_ref, m_scr, acc_scr, *, tq, hps):
    s_idx = pl.program_id(2)
    qi = qi_ref[s_idx]
    ki = ki_ref[s_idx]

    @pl.when(ki == 0)
    def _():
        m_scr[...] = jnp.full_like(m_scr, -jnp.inf)
        acc_scr[...] = jnp.zeros_like(acc_scr)

    def body(masked):
        reps = tq // LANES
        lane_lo = _iota2((tq, LANES), 1) < FOX_HEAD_DIM
        if masked:
            keep = _iota2((tq, tq), 1) <= _iota2((tq, tq), 0)
        for hh in range(hps):
            pair = slice((hh // 2) * LANES, (hh // 2 + 1) * LANES)
            own = lane_lo if hh % 2 == 0 else jnp.logical_not(lane_lo)
            v1 = jnp.where(own, v_ref[0, :, pair], 1.0).astype(BF16)
            s = _dot_nt(q_ref[0, hh], k_ref[0, hh])
            if masked:
                s = jnp.where(keep, s, -jnp.inf)
            m_prev = m_scr[hh]
            m_new = jnp.maximum(m_prev, jnp.max(s, axis=1, keepdims=True))
            p = jnp.exp2(s - jnp.tile(m_new, (1, reps)))
            acc_scr[hh] = jnp.exp2(m_prev - m_new) * acc_scr[hh] + _dot(p.astype(BF16), v1)
            m_scr[hh] = m_new

    pl.when(ki < qi)(lambda: body(False))

    @pl.when(ki == qi)
    def _():
        body(True)
        lane_lo = _iota2((tq, LANES), 1) < FOX_HEAD_DIM
        for pp in range(hps // 2):
            a0, a1 = acc_scr[2 * pp], acc_scr[2 * pp + 1]
            o0 = a0 / pltpu.roll(a0, FOX_HEAD_DIM, axis=1)
            o1 = a1 / pltpu.roll(a1, FOX_HEAD_DIM, axis=1)
            o_ref[0, :, pp * LANES:(pp + 1) * LANES] = jnp.where(lane_lo, o0, o1).astype(o_ref.dtype)


def _fox_prompt(q_aug, k_aug, v, *, tq, hps):
    b, nh, l, _ = q_aug.shape
    nq = l // tq
    vw = hps * FOX_HEAD_DIM
    qi = np.array([i for i in range(nq) for _ in range(i + 1)], np.int32)
    ki = np.array([j for i in range(nq) for j in range(i + 1)], np.int32)
    grid_spec = pltpu.PrefetchScalarGridSpec(
        num_scalar_prefetch=2,
        grid=(b, nh // hps, len(qi)),
        in_specs=[
            pl.BlockSpec((1, hps, tq, LANES), lambda i, p, s, qi, ki: (i, p, qi[s], 0)),
            pl.BlockSpec((1, hps, tq, LANES), lambda i, p, s, qi, ki: (i, p, ki[s], 0)),
            pl.BlockSpec((1, tq, vw), lambda i, p, s, qi, ki: (i, ki[s], p)),
        ],
        out_specs=pl.BlockSpec((1, tq, vw), lambda i, p, s, qi, ki: (i, qi[s], p)),
        scratch_shapes=[pltpu.VMEM((hps, tq, LANES), F32), pltpu.VMEM((hps, tq, LANES), F32)],
    )
    return pl.pallas_call(
        functools.partial(_fox_prompt_kernel, tq=tq, hps=hps),
        grid_spec=grid_spec,
        out_shape=jax.ShapeDtypeStruct((b, l, nh * FOX_HEAD_DIM), BF16),
        compiler_params=_params("parallel", "parallel", "arbitrary"),
        name="fox_prompt",
    )(jnp.asarray(qi), jnp.asarray(ki), q_aug, k_aug, v)


def _gate_scan_kernel(lf_ref, u_ref, o_ref):
    o_ref[...] = _dot(jnp.concatenate(_split3(lf_ref[...]), axis=1), u_ref[...])


def _gate_scan(lf_rows, *, rows):
    n, page = lf_rows.shape
    later = np.triu(np.ones((page, page), np.float32), 1).T
    u3 = jnp.asarray(np.tile(np.concatenate([later, np.ones((page, page), np.float32)], axis=1), (3, 1)))
    return pl.pallas_call(
        _gate_scan_kernel,
        grid=(n // rows,),
        in_specs=[pl.BlockSpec((rows, page), lambda i: (i, 0)), pl.BlockSpec((3 * page, 2 * page), lambda i: (0, 0))],
        out_specs=pl.BlockSpec((rows, 2 * page), lambda i: (i, 0)),
        out_shape=jax.ShapeDtypeStruct((n, 2 * page), F32),
        compiler_params=_params("parallel"),
        name="gate_scan",
    )(lf_rows, u3)


def _fox_sample_kernel(pt_ref, q_ref, kn_ref, vn_ref, small_ref, bf_ref, *rest, PG, page, nq, f_off):
    kt_refs = rest[:PG]
    vt_refs = rest[PG:2 * PG]
    gl_refs = rest[2 * PG:3 * PG]
    o_ref, logf_ref, qx_scr, m_scr, l_scr, acc_scr, carry_scr = rest[3 * PG:]
    nh, dh = FOX_N_HEADS, FOX_HEAD_DIM
    rows = nq * nh
    width = nh * dh
    g = pl.program_id(1)
    own = (_iota2((rows, width), 1) // dh) == (_iota2((rows, width), 0) % nh)

    logf_new = -_softplus(-(small_ref[0][:, f_off:f_off + nh] + bf_ref[...]))
    lf8 = jnp.where(_iota2((8, nh), 0) < nq, logf_new, 0.0)
    tri8 = (_iota2((8, 8), 0) <= _iota2((8, 8), 1)).astype(F32)
    fnew = _dot_tn(tri8, lf8, precision=HIGHEST)
    fnewT = _dot_tn(lf8, tri8, precision=HIGHEST)
    frow = jnp.concatenate([jnp.broadcast_to(fnew[qq:qq + 1, :], (nh, nh)) for qq in range(nq)], axis=0)
    fcol = jnp.sum(jnp.where(_iota2((rows, nh), 1) == _iota2((rows, nh), 0) % nh, frow, 0.0), axis=1, keepdims=True)

    @pl.when(g == 0)
    def _():
        qrows = jnp.concatenate([jnp.broadcast_to(q_ref[0][qq:qq + 1, :], (nh, width)) for qq in range(nq)], axis=0)
        qx_scr[...] = jnp.where(own, qrows, 0.0)
        m_scr[...] = jnp.full_like(m_scr, -jnp.inf)
        l_scr[...] = jnp.zeros_like(l_scr)
        acc_scr[...] = jnp.zeros_like(acc_scr)
        carry_scr[...] = jnp.zeros_like(carry_scr)
        logf_ref[0] = logf_new

    qx = qx_scr[...]

    scores = []
    carry = carry_scr[...]
    for i in range(PG):
        gl = gl_refs[i][...]
        gate = carry + gl[:, :page]
        carry = carry + gl[:, page:]
        scores.append(_dot(qx, kt_refs[i][0]) + fcol + jnp.concatenate([gate] * nq, axis=0))
    carry_scr[...] = carry
    m_prev = m_scr[...]
    m_new = m_prev
    for s in scores:
        m_new = jnp.maximum(m_new, jnp.max(s, axis=1, keepdims=True))
    alpha = jnp.exp(m_prev - m_new)
    l_new = alpha * l_scr[...]
    acc = alpha * acc_scr[...]
    for i, s in enumerate(scores):
        p = jnp.exp(s - m_new)
        l_new = l_new + jnp.sum(p, axis=1, keepdims=True)
        acc = acc + _dot_nt(p, vt_refs[i][0])
    m_scr[...] = m_new
    l_scr[...] = l_new
    acc_scr[...] = acc

    @pl.when(g == pl.num_programs(1) - 1)
    def _():
        s = _dot_nt(qx, kn_ref[0])
        s = s + fcol - jnp.concatenate([fnewT] * nq, axis=0)
        keep = _iota2((rows, 8), 1) <= _iota2((rows, 8), 0) // nh
        s = jnp.where(keep, s, -jnp.inf)
        m_fin = jnp.maximum(m_scr[...], jnp.max(s, axis=1, keepdims=True))
        a_fin = jnp.exp(m_scr[...] - m_fin)
        p = jnp.exp(s - m_fin)
        l_fin = a_fin * l_scr[...] + jnp.sum(p, axis=1, keepdims=True)
        out = jnp.where(own, (a_fin * acc_scr[...] + _dot(p, vn_ref[0])) / l_fin, 0.0)
        pick = (_iota2((8, rows), 1) // nh == _iota2((8, rows), 0)).astype(F32)
        o_ref[0] = _dot(pick, out, precision=HIGHEST).astype(o_ref.dtype)


def _fox_sample(q, k_new, v_new, small, b_fox_f, cache_kt, cache_vt, gates, page_table, *, nq, PG, f_off):
    b, _, width = q.shape
    nh = FOX_N_HEADS
    n_pages = page_table.shape[1]
    page = cache_kt.shape[2]
    assert n_pages % PG == 0
    rows = nq * nh

    def pmap(i):
        return lambda bb, g, pt: (pt[bb, n_pages - 1 - (g * PG + i)], 0, 0)

    def gmap(i):
        return lambda bb, g, pt: (pt[bb, n_pages - 1 - (g * PG + i)], 0)

    fixed = lambda shape: pl.BlockSpec(shape, lambda bb, g, pt: (bb,) + (0,) * (len(shape) - 1))
    in_specs = [fixed((1, 8, width)), fixed((1, 8, width)), fixed((1, 8, width)), fixed((1, 8, LANES)),
                pl.BlockSpec((1, nh), lambda bb, g, pt: (0, 0))]
    in_specs += [pl.BlockSpec((1, width, page), pmap(i)) for i in range(PG)]
    in_specs += [pl.BlockSpec((1, width, page), pmap(i)) for i in range(PG)]
    in_specs += [pl.BlockSpec((nh, 2 * page), gmap(i)) for i in range(PG)]
    grid_spec = pltpu.PrefetchScalarGridSpec(
        num_scalar_prefetch=1,
        grid=(b, n_pages // PG),
        in_specs=in_specs,
        out_specs=[fixed((1, 8, width)), fixed((1, 8, nh))],
        scratch_shapes=[pltpu.VMEM((rows, width), F32), pltpu.VMEM((rows, 1), F32), pltpu.VMEM((rows, 1), F32),
                        pltpu.VMEM((rows, width), F32), pltpu.VMEM((nh, page), F32)],
    )
    return pl.pallas_call(
        functools.partial(_fox_sample_kernel, PG=PG, page=page, nq=nq, f_off=f_off),
        grid_spec=grid_spec,
        out_shape=[jax.ShapeDtypeStruct((b, 8, width), F32), jax.ShapeDtypeStruct((b, 8, nh), F32)],
        compiler_params=_params("parallel", "arbitrary"),
        name="fox_sample",
    )(page_table, q, k_new, v_new, small, b_fox_f.reshape(1, nh),
      *([cache_kt] * PG), *([cache_vt] * PG), *([gates] * PG))


def _mix_kernel(x_ref, ys_ref, yf_ref, gt_ref, bg_ref, wso_ref, wfo_ref, wmx_ref, nm_ref, wq_ref, x1_ref, qm_ref, *, qscale):
    d = x_ref.shape[-1]
    gate = _sigmoid(gt_ref[...] + bg_ref[...])
    merged = gate[:, :d] * _dot(ys_ref[...], wso_ref[...]) + gate[:, d:] * _dot(yf_ref[...], wfo_ref[...])
    x1 = x_ref[...] + _dot(merged.astype(BF16), wmx_ref[...])
    x1_ref[...] = x1
    h = _rms(x1, nm_ref[...]).astype(BF16)
    qm_ref[...] = (_dot(h, wq_ref[...]) * qscale).astype(qm_ref.dtype)


def _mix(x, y_ssd, y_fox, gates, b_gate, w_so, w_fo, w_mx, norm_mem, w_q, *, tm, qscale, q_dtype):
    m, d = x.shape
    row = lambda w: pl.BlockSpec((tm, w), lambda i: (i, 0))
    full = lambda a: pl.BlockSpec(a.shape, lambda i: (0,) * a.ndim)
    bg, nm = b_gate.reshape(1, -1), norm_mem.reshape(1, -1)
    return pl.pallas_call(
        functools.partial(_mix_kernel, qscale=qscale),
        grid=(m // tm,),
        in_specs=[row(d), row(y_ssd.shape[1]), row(d), row(2 * d), full(bg), full(w_so), full(w_fo), full(w_mx), full(nm), full(w_q)],
        out_specs=[row(d), row(d)],
        out_shape=[jax.ShapeDtypeStruct((m, d), F32), jax.ShapeDtypeStruct((m, d), q_dtype)],
        compiler_params=_params("parallel"),
        name="mix",
    )(x, y_ssd, y_fox, gates, bg, w_so, w_fo, w_mx, nm, w_q)


def _mem_attn_kernel(q_ref, k_ref, v_ref, o_ref, *, cast):
    dh = q_ref.shape[-1] // MEM_N_HEADS
    for h in range(MEM_N_HEADS):
        cols = slice(h * dh, (h + 1) * dh)
        s = _dot_nt(cast(q_ref[0][:, cols]), cast(k_ref[0][:, cols]))
        p = jnp.exp(s - jnp.max(s, axis=1, keepdims=True))
        p = p / jnp.sum(p, axis=1, keepdims=True)
        o_ref[0, :, cols] = _dot(cast(p), cast(v_ref[0][:, cols])).astype(o_ref.dtype)


def _mem_attn(q, mem_k, mem_v, *, tl):
    b, l, d = q.shape
    m = mem_k.shape[1]
    cast = (lambda v: v.astype(BF16)) if q.dtype == BF16 else (lambda v: v)
    return pl.pallas_call(
        functools.partial(_mem_attn_kernel, cast=cast),
        grid=(b, l // tl),
        in_specs=[pl.BlockSpec((1, tl, d), lambda i, s: (i, s, 0)),
                  pl.BlockSpec((1, m, d), lambda i, s: (i, 0, 0)),
                  pl.BlockSpec((1, m, d), lambda i, s: (i, 0, 0))],
        out_specs=pl.BlockSpec((1, tl, d), lambda i, s: (i, s, 0)),
        out_shape=jax.ShapeDtypeStruct((b, l, d), q.dtype),
        compiler_params=_params("parallel", "parallel"),
        name="mem_attn",
    )(q, mem_k, mem_v)


def _route_kernel(x1_ref, att_ref, wo_ref, nf_ref, wr_ref, br_ref, x2_ref, h_ref, rt_ref):
    x2 = x1_ref[...] + _dot(att_ref[...], wo_ref[...])
    x2_ref[...] = x2
    h = _rms(x2, nf_ref[...])
    h_ref[...] = h.astype(h_ref.dtype)
    h_hi = h.astype(BF16)
    h_lo = (h - h_hi.astype(F32)).astype(BF16)
    logits = _dot(jnp.concatenate([h_hi, h_hi, h_lo], axis=1), wr_ref[...]) + br_ref[...]
    lane = _iota2(logits.shape, 1).astype(F32)
    first = lambda mask: jnp.min(jnp.where(mask, lane, float(LANES)), axis=1, keepdims=True)
    gl = jnp.where(lane < N_EXPERT_GROUPS, logits, -jnp.inf)
    gmax = jnp.max(gl, axis=1, keepdims=True)
    g_idx = first(gl == gmax)
    g_w = 1.0 / jnp.sum(jnp.exp(gl - gmax), axis=1, keepdims=True)
    e_lo = N_EXPERT_GROUPS + g_idx * EXPERTS_PER_GROUP
    el = jnp.where((lane >= e_lo) & (lane < e_lo + EXPERTS_PER_GROUP), logits, -jnp.inf)
    v1 = jnp.max(el, axis=1, keepdims=True)
    i1 = first(el == v1)
    el2 = jnp.where(lane == i1, -jnp.inf, el)
    v2 = jnp.max(el2, axis=1, keepdims=True)
    i2 = first(el2 == v2)
    e21 = jnp.exp(v2 - v1)
    w1 = g_w / (1.0 + e21)
    w2 = g_w * e21 / (1.0 + e21)
    rt_ref[...] = jnp.where(lane == 0, i1 - N_EXPERT_GROUPS, jnp.where(lane == 1, i2 - N_EXPERT_GROUPS,
                            jnp.where(lane == 2, w1, jnp.where(lane == 3, w2, 0.0))))


def _route(x1, att, w_o, norm_ffn, w_router, b_router, *, tm):
    m, d = x1.shape
    row = lambda w: pl.BlockSpec((tm, w), lambda i: (i, 0))
    full = lambda a: pl.BlockSpec(a.shape, lambda i: (0,) * a.ndim)
    nf = norm_ffn.reshape(1, d)
    return pl.pallas_call(
        _route_kernel,
        grid=(m // tm,),
        in_specs=[row(d), row(d), full(w_o), full(nf), full(w_router), full(b_router)],
        out_specs=[row(d), row(d), row(LANES)],
        out_shape=[jax.ShapeDtypeStruct((m, d), F32), jax.ShapeDtypeStruct((m, d), F32), jax.ShapeDtypeStruct((m, LANES), F32)],
        compiler_params=_params("parallel"),
        name="route",
    )(x1, att, w_o, nf, w_router, b_router)


def _expert_kernel(te_ref, nt_ref, x_ref, wg_ref, wu_ref, wd_ref, o_ref):
    @pl.when(pl.program_id(0) < nt_ref[0])
    def _():
        x = x_ref[...].astype(BF16)
        hid = _silu(_dot(x, wg_ref[0].astype(BF16))) * _dot(x, wu_ref[0].astype(BF16))
        o_ref[...] = _dot(hid.astype(BF16), wd_ref[0].astype(BF16)).astype(o_ref.dtype)

    @pl.when(pl.program_id(0) >= nt_ref[0])
    def _():
        o_ref[...] = jnp.zeros_like(o_ref)


def _experts(x_sorted, tile_expert, n_tiles, w_gate, w_up, w_down, *, tm):
    p, d = x_sorted.shape
    ne, _, ff = w_gate.shape
    grid_spec = pltpu.PrefetchScalarGridSpec(
        num_scalar_prefetch=2,
        grid=(p // tm,),
        in_specs=[pl.BlockSpec((tm, d), lambda i, te, nt: (i, 0)),
                  pl.BlockSpec((1, d, ff), lambda i, te, nt: (te[i], 0, 0)),
                  pl.BlockSpec((1, d, ff), lambda i, te, nt: (te[i], 0, 0)),
                  pl.BlockSpec((1, ff, d), lambda i, te, nt: (te[i], 0, 0))],
        out_specs=pl.BlockSpec((tm, d), lambda i, te, nt: (i, 0)),
    )
    return pl.pallas_call(
        _expert_kernel,
        grid_spec=grid_spec,
        out_shape=jax.ShapeDtypeStruct((p, d), F32),
        compiler_params=_params("arbitrary"),
        name="experts",
    )(tile_expert, n_tiles, x_sorted, w_gate, w_up, w_down)


def _experts_dense_kernel(h_ref, rt_ref, wg_ref, wu_ref, wd_ref, o_ref):
    e = pl.program_id(0)

    @pl.when(e == 0)
    def _():
        o_ref[...] = jnp.zeros_like(o_ref)

    rt = rt_ref[...]
    ef = e.astype(F32)
    cw = jnp.where(rt[:, 0:1] == ef, rt[:, 2:3], 0.0) + jnp.where(rt[:, 1:2] == ef, rt[:, 3:4], 0.0)
    x = h_ref[...].astype(BF16)
    hid = _silu(_dot(x, wg_ref[0].astype(BF16))) * _dot(x, wu_ref[0].astype(BF16))
    o_ref[...] = o_ref[...] + cw * _dot(hid.astype(BF16), wd_ref[0].astype(BF16))


def _experts_dense(h, rt, w_gate, w_up, w_down):
    m, d = h.shape
    ne, _, ff = w_gate.shape
    return pl.pallas_call(
        _experts_dense_kernel,
        grid=(ne,),
        in_specs=[pl.BlockSpec((m, d), lambda e: (0, 0)), pl.BlockSpec((m, LANES), lambda e: (0, 0)),
                  pl.BlockSpec((1, d, ff), lambda e: (e, 0, 0)), pl.BlockSpec((1, d, ff), lambda e: (e, 0, 0)),
                  pl.BlockSpec((1, ff, d), lambda e: (e, 0, 0))],
        out_specs=pl.BlockSpec((m, d), lambda e: (0, 0)),
        out_shape=jax.ShapeDtypeStruct((m, d), F32),
        compiler_params=_params("arbitrary"),
        name="experts_dense",
    )(h, rt, w_gate, w_up, w_down)


def _final_kernel(x2_ref, ya_ref, yb_ref, rt_ref, g_ref, o_ref):
    rt = rt_ref[...]
    moe = rt[:, 2:3] * ya_ref[...] + rt[:, 3:4] * yb_ref[...]
    o_ref[...] = _rms(x2_ref[...] + moe, g_ref[...])


def _final(x2, ya, yb, rt, norm_final, *, tm):
    m, d = x2.shape
    row = lambda w: pl.BlockSpec((tm, w), lambda i: (i, 0))
    return pl.pallas_call(
        _final_kernel,
        grid=(m // tm,),
        in_specs=[row(d), row(d), row(d), row(LANES), pl.BlockSpec((1, d), lambda i: (0, 0))],
        out_specs=row(d),
        out_shape=jax.ShapeDtypeStruct((m, d), F32),
        compiler_params=_params("parallel"),
        name="final",
    )(x2, ya, yb, rt, norm_final.reshape(1, d))


def _final_dense_kernel(x2_ref, moe_ref, g_ref, o_ref):
    o_ref[...] = _rms(x2_ref[...] + moe_ref[...], g_ref[...])


def _final_dense(x2, moe, norm_final):
    m, d = x2.shape
    return pl.pallas_call(
        _final_dense_kernel,
        out_shape=jax.ShapeDtypeStruct((m, d), F32),
        name="final_dense",
    )(x2, moe, norm_final.reshape(1, d))


def _dispatch_plan(eid, tm):
    n = eid.shape[0]
    p = (n + N_EXPERTS * (tm - 1) + tm - 1) // tm * tm
    onehot = (eid[:, None] == jnp.arange(N_EXPERTS, dtype=jnp.int32)[None, :]).astype(jnp.int32)
    csum = jnp.cumsum(onehot, axis=0)
    counts = csum[-1]
    rank = jnp.sum(csum * onehot, axis=1) - 1
    padded = (counts + tm - 1) // tm * tm
    pend = jnp.cumsum(padded)
    pstart = pend - padded
    pos = jnp.sum(pstart[None, :] * onehot, axis=1) + rank
    src = jnp.full((p,), -1, jnp.int32).at[pos].set(jnp.arange(n, dtype=jnp.int32), unique_indices=True)
    n_tiles = (pend[-1] // tm).astype(jnp.int32)
    tile_first = jnp.arange(p // tm, dtype=jnp.int32) * tm
    tile_expert = jnp.sum((tile_first[:, None] >= pend[None, :]).astype(jnp.int32), axis=1)
    last_used = jnp.sum((jnp.maximum(pend[-1] - tm, 0) >= pend).astype(jnp.int32))
    tile_expert = jnp.where(tile_first < pend[-1], tile_expert, last_used).astype(jnp.int32)
    return src, pos, tile_expert, n_tiles.reshape(1)


def _bf(w):
    return w.astype(BF16)


def kernel(x_prompt, x_sample, mem_prompt, cache_fox_k, cache_fox_v, cache_fox_logf, page_table, cache_mem_k, cache_mem_v, state_conv, state_ssm, norm_mix, w_in, conv_w, conv_b, dt_bias, a_log, d_skip, ssd_norm, w_ssd_out, b_fox_f, w_fox_out, b_gate, w_mix_out, norm_mem, norm_mem_kv, w_mem_q, w_mem_k, w_mem_v, w_mem_o, norm_ffn, w_router_group, b_router_group, w_router_expert, b_router_expert, w_exp_gate, w_exp_up, w_exp_down, norm_final):
    depth = w_in.shape[0]
    assert depth == 1, "single-layer step"
    bp, sp, d = x_prompt.shape
    bs, ss, _ = x_sample.shape
    tp, ts = bp * sp, bs * ss
    d_inner = ssd_norm.shape[-1]
    cdim = conv_w.shape[-1]
    nh_ssd = dt_bias.shape[-1]
    nh, dh = FOX_N_HEADS, FOX_HEAD_DIM
    fw = nh * dh
    mem_len = mem_prompt.shape[1]
    l = 0

    o_z, o_x, o_dt, o_q, o_k, o_v, o_f, o_g = np.cumsum([0, d_inner, cdim, nh_ssd, fw, fw, fw, nh]).tolist()
    wi = w_in[l]
    tn = 1024
    seg = lambda a, b: _tile_cols(_bf(wi[:, a:b]), tn)
    w_z, w_x, w_qf, w_kf, w_vf, w_g = seg(o_z, o_x), seg(o_x, o_dt), seg(o_q, o_k), seg(o_k, o_v), seg(o_v, o_f), seg(o_g, o_g + 2 * d)
    n_small = nh_ssd + nh
    w_small = _bf(jnp.pad(jnp.concatenate([wi[:, o_dt:o_q], wi[:, o_f:o_g]], axis=1), ((0, 0), (0, LANES - n_small))))[None]
    f_off = nh_ssd
    w_so, w_fo, w_mx, w_q, w_o = _bf(w_ssd_out[l]), _bf(w_fox_out[l]), _bf(w_mix_out[l]), _bf(w_mem_q[l]), _bf(w_mem_o[l])
    n_r = N_EXPERT_GROUPS + N_EXPERTS
    w_router = jnp.pad(jnp.concatenate([w_router_group[l], w_router_expert[l]], axis=1), ((0, 0), (0, LANES - n_r)))
    wr_hi = _bf(w_router)
    wr_lo = _bf(w_router - wr_hi.astype(F32))
    w_router = jnp.concatenate([wr_hi, wr_lo, wr_hi], axis=0)
    b_router = jnp.pad(jnp.concatenate([b_router_group[l], b_router_expert[l]]), (0, LANES - n_r)).reshape(1, LANES)
    mem_scale = (d // MEM_N_HEADS) ** -0.5

    def in_proj(x2d, tm, q_dtype, q_scale):
        h = _rmsnorm(x2d, norm_mix[l], tm)
        mm = lambda w, dt=F32, sc=1.0: _matmul(h, w, dt, tm=tm, scale=sc)
        return (mm(w_z), mm(w_x), mm(w_qf, q_dtype, q_scale), mm(w_kf), mm(w_vf), mm(w_g), mm(w_small)), h

    def post(x2d, y_ssd, y_fox, gates, mem_k, mem_v, nb, tm, tl):
        per = x2d.shape[0] // nb
        q_dtype = BF16 if per % 16 == 0 else F32
        x1, qm = _mix(x2d, y_ssd, y_fox, gates, b_gate[l], w_so, w_fo, w_mx, norm_mem[l], w_q, tm=tm, qscale=mem_scale, q_dtype=q_dtype)
        qm = qm.reshape(nb, per, d)
        if per % 8:
            qm = jnp.pad(qm, ((0, 0), (0, 8 - per % 8), (0, 0)))
        att = _mem_attn(qm, mem_k, mem_v, tl=tl)[:, :per].reshape(-1, d).astype(BF16)
        return _route(x1, att, w_o, norm_ffn[l], w_router, b_router, tm=tm)

    xp = x_prompt.reshape(tp, d)
    (z, xbc, q, k, v, gates, small), h_in = in_proj(xp, 1024, BF16, dh ** -0.5 * LOG2E)
    k_out = _matmul_t(_bf(wi[:, o_k:o_v]).T[None], h_in, seqs=bp, tm=1024).reshape(bp, nh, dh, sp).transpose(0, 3, 1, 2)
    v_out = _matmul_t(_bf(wi[:, o_v:o_f]).T[None], h_in, seqs=bp, tm=1024).reshape(bp, nh, dh, sp).transpose(0, 3, 1, 2)
    conv0 = jnp.zeros((bp, SSD_CONV - 1, cdim), F32)
    ssm0 = jnp.zeros((bp, nh_ssd, SSD_HEAD_DIM, SSD_D_STATE), F32)
    y_ssd, pconv, pssm = _ssd(xbc.reshape(bp, sp, cdim), z.reshape(bp, sp, d_inner), small.reshape(bp, sp, LANES), conv0, ssm0,
                              conv_w[l], conv_b[l], dt_bias[l], a_log[l], d_skip[l], ssd_norm[l], T=128, CPS=4, valid=None)
    plogf, q_aug, k_aug = _fox_gate(small.reshape(bp, sp, LANES), q.reshape(bp, sp, fw), k.reshape(bp, sp, fw), b_fox_f[l], tr=512, f_off=f_off)
    y_fox = _fox_prompt(q_aug, k_aug, v.reshape(bp, sp, fw), tq=1024, hps=4)
    mem_h = _rmsnorm(mem_prompt.reshape(bp * mem_len, d), norm_mem_kv[l], 256)
    mk = _matmul(mem_h, _tile_cols(_bf(w_mem_k[l]), 512), F32, tm=256)
    mv = _matmul(mem_h, _tile_cols(_bf(w_mem_v[l]), 512), F32, tm=256)
    x2p, hp, rtp = post(xp, y_ssd.reshape(tp, d_inner), y_fox.reshape(tp, fw), gates, mk.reshape(bp, mem_len, d), mv.reshape(bp, mem_len, d), bp, 256, 512)

    tm_e = 256
    eid = rtp[:, :2].astype(jnp.int32).reshape(-1)
    src, pos, tile_expert, n_tiles = _dispatch_plan(eid, tm_e)
    take = lambda a, idx: a.at[idx].get(mode="promise_in_bounds")
    src, x_sample = lax.optimization_barrier((src, x_sample))
    x_sorted = take(hp, jnp.maximum(src, 0) // 2)

    xs = x_sample.reshape(ts, d)
    (zs, xbcs, qs, ks, vs, gates_s, small_s), _ = in_proj(xs, ts, F32, dh ** -0.5)
    pad8 = lambda a: jnp.pad(a.reshape(bs, ss, -1), ((0, 0), (0, 8 - ss), (0, 0)))
    ys_ssd, sconv, sssm = _ssd(pad8(xbcs), pad8(zs), pad8(small_s), state_conv[l], state_ssm[l],
                               conv_w[l], conv_b[l], dt_bias[l], a_log[l], d_skip[l], ssd_norm[l], T=8, CPS=1, valid=ss)
    pool, page = cache_fox_k.shape[1], cache_fox_k.shape[2]
    kt = jnp.transpose(cache_fox_k[l], (0, 2, 3, 1)).reshape(pool, fw, page)
    vt = jnp.transpose(cache_fox_v[l], (0, 2, 3, 1)).reshape(pool, fw, page)
    gates_past = _gate_scan(jnp.transpose(cache_fox_logf[l], (0, 2, 1)).reshape(pool * nh, page), rows=2048)
    ys_fox, slogf = _fox_sample(pad8(qs), pad8(ks), pad8(vs), pad8(small_s), b_fox_f[l], kt, vt, gates_past, page_table,
                                nq=ss, PG=16, f_off=f_off)
    x2s, hs, rts = post(xs, ys_ssd[:, :ss].reshape(ts, d_inner).astype(BF16), ys_fox[:, :ss].reshape(ts, fw).astype(BF16), gates_s,
                        cache_mem_k[l].reshape(bs, mem_len, d), cache_mem_v[l].reshape(bs, mem_len, d), bs, ts, 8)

    moe_s = _experts_dense(hs, rts, w_exp_gate[l], w_exp_up[l], w_exp_down[l])
    y_sample = _final_dense(x2s, moe_s, norm_final).reshape(bs, ss, d)

    x_sorted, y_sample = lax.optimization_barrier((x_sorted, y_sample))
    y_sorted = _experts(x_sorted, tile_expert, n_tiles, w_exp_gate[l], w_exp_up[l], w_exp_down[l], tm=tm_e)
    pos2 = pos.reshape(tp, 2)
    y_prompt = _final(x2p, take(y_sorted, pos2[:, 0]), take(y_sorted, pos2[:, 1]), rtp, norm_final, tm=512).reshape(bp, sp, d)

    return (y_prompt, y_sample,
            k_out[None], v_out[None], plogf.reshape(1, bp, sp, nh),
            mk.reshape(1, bp, mem_len, MEM_N_HEADS, d // MEM_N_HEADS), mv.reshape(1, bp, mem_len, MEM_N_HEADS, d // MEM_N_HEADS),
            pconv[None], pssm[None],
            ks.reshape(1, bs, ss, nh, dh), vs.reshape(1, bs, ss, nh, dh), slogf[:, :ss][None],
            sconv[None], sssm[None])
```

```python
import functools

import numpy as np
import jax
import jax.numpy as jnp
from jax import lax
from jax.experimental import pallas as pl
from jax.experimental.pallas import tpu as pltpu

F32 = jnp.float32
BF16 = jnp.bfloat16
HIGHEST = lax.Precision.HIGHEST

RMS_EPS = 1e-6
LOG2E = 1.4426950408889634
SSD_HEAD_DIM = 64
SSD_N_GROUPS = 4
SSD_D_STATE = 128
SSD_CONV = 4
FOX_N_HEADS = 16
FOX_HEAD_DIM = 64
MEM_N_HEADS = 4
N_EXPERT_GROUPS = 4
EXPERTS_PER_GROUP = 8
N_EXPERTS = N_EXPERT_GROUPS * EXPERTS_PER_GROUP

LANES = 128
VMEM_LIMIT = 56 * 1024 * 1024


def _params(*sem):
    return pltpu.CompilerParams(dimension_semantics=sem, vmem_limit_bytes=VMEM_LIMIT)


def _dot(a, b, **kw):
    return jnp.dot(a, b, preferred_element_type=F32, **kw)


def _dot_nt(a, b, **kw):
    return lax.dot_general(a, b, (((1,), (1,)), ((), ())), preferred_element_type=F32, **kw)


def _dot_tn(a, b, **kw):
    return lax.dot_general(a, b, (((0,), (0,)), ((), ())), preferred_element_type=F32, **kw)


def _softplus(x):
    return jnp.maximum(x, 0.0) + jnp.log1p(jnp.exp(-jnp.abs(x)))


def _sigmoid(x):
    return 1.0 / (1.0 + jnp.exp(-x))


def _silu(x):
    return x * _sigmoid(x)


def _rms(x, g):
    return x * lax.rsqrt(jnp.mean(x * x, axis=-1, keepdims=True) + RMS_EPS) * g


def _split3(f):
    hi = f.astype(BF16).astype(F32)
    r = f - hi
    mid = r.astype(BF16).astype(F32)
    lo = (r - mid).astype(BF16).astype(F32)
    return hi, mid, lo


def _iota2(shape, dim):
    return lax.broadcasted_iota(jnp.int32, shape, dim)


def _rmsnorm_kernel(x_ref, g_ref, o_ref):
    o_ref[...] = _rms(x_ref[...].astype(F32), g_ref[...]).astype(o_ref.dtype)


def _rmsnorm(x, g, tm):
    m, d = x.shape
    return pl.pallas_call(
        _rmsnorm_kernel,
        grid=(m // tm,),
        in_specs=[pl.BlockSpec((tm, d), lambda i: (i, 0)), pl.BlockSpec((1, d), lambda i: (0, 0))],
        out_specs=pl.BlockSpec((tm, d), lambda i: (i, 0)),
        out_shape=jax.ShapeDtypeStruct((m, d), BF16),
        compiler_params=_params("parallel"),
        name="rmsnorm",
    )(x, g.reshape(1, d))


def _mm_kernel(a_ref, w_ref, o_ref, *, scale):
    acc = _dot(a_ref[...], w_ref[0])
    if scale != 1.0:
        acc = acc * jnp.where(pl.program_id(1) == 0, scale, 1.0)
    o_ref[...] = acc.astype(o_ref.dtype)


def _tile_cols(w, tn):
    k, n = w.shape
    return jnp.swapaxes(w.reshape(k, n // tn, tn), 0, 1)


def _matmul(a, w_tiles, out_dtype, *, tm, scale=1.0):
    m, k = a.shape
    nt, _, tn = w_tiles.shape
    return pl.pallas_call(
        functools.partial(_mm_kernel, scale=scale),
        grid=(m // tm, nt),
        in_specs=[pl.BlockSpec((tm, k), lambda i, j: (i, 0)), pl.BlockSpec((1, k, tn), lambda i, j: (j, 0, 0))],
        out_specs=pl.BlockSpec((tm, tn), lambda i, j: (i, j)),
        out_shape=jax.ShapeDtypeStruct((m, nt * tn), out_dtype),
        compiler_params=_params("parallel", "arbitrary"),
        name="matmul",
    )(a, w_tiles)


def _mm_t_kernel(w_ref, a_ref, o_ref):
    o_ref[0] = _dot_nt(w_ref[0], a_ref[...])


def _matmul_t(w_rows, a, *, seqs, tm):
    m, k = a.shape
    nt, tn, _ = w_rows.shape
    per = m // seqs // tm
    return pl.pallas_call(
        _mm_t_kernel,
        grid=(m // tm, nt),
        in_specs=[pl.BlockSpec((1, tn, k), lambda i, j: (j, 0, 0)), pl.BlockSpec((tm, k), lambda i, j: (i, 0))],
        out_specs=pl.BlockSpec((1, tn, tm), lambda i, j: (i // per, j, i % per)),
        out_shape=jax.ShapeDtypeStruct((seqs, nt * tn, m // seqs), F32),
        compiler_params=_params("parallel", "arbitrary"),
        name="matmul_t",
    )(w_rows, a)


def _ssd_kernel(xbc_ref, z_ref, small_ref, dtT_ref, conv0_ref, ssm0_ref,
                convw_ref, convb_ref, dtb_ref, dtbc_ref, alog_ref, alogc_ref, dskip_ref, gnorm_ref, e_ref,
                y_ref, convn_ref, ssm_ref, convbuf, ybuf, *, T, CPS, valid, n_heads, cast):
    step = pl.program_id(1)
    d_inner = n_heads * SSD_HEAD_DIM
    gw = d_inner // SSD_N_GROUPS
    hpg = n_heads // SSD_N_GROUPS
    n = SSD_D_STATE
    k1 = SSD_CONV - 1

    @pl.when(step == 0)
    def _():
        ssm_ref[...] = ssm0_ref[...]
        convbuf[8 - k1:8, :] = conv0_ref[0]

    row_t = _iota2((T, T), 0)
    col_t = _iota2((T, T), 1)
    causal = row_t >= col_t
    lmat = causal.astype(F32)
    umat = (row_t <= col_t).astype(F32)
    lane_lo = _iota2((T, LANES), 1) < SSD_HEAD_DIM
    row_lo = _iota2((LANES, 1), 0) < SSD_HEAD_DIM
    a_row = -jnp.exp(alog_ref[...])
    a_col = -jnp.exp(alogc_ref[...])
    emat = e_ref[...]

    def chunk(c, carry):
        r0 = pl.multiple_of(c * T, T)
        raw = xbc_ref[0, pl.ds(r0, T), :]
        convbuf[8:8 + T, :] = raw
        acc = convb_ref[...] + convbuf[8 - k1:8 - k1 + T, :] * convw_ref[0:1, :]
        for j in range(1, SSD_CONV):
            acc = acc + convbuf[8 - k1 + j:8 - k1 + j + T, :] * convw_ref[j:j + 1, :]
        if valid is None:
            new_conv = raw[T - k1:T, :]
        else:
            new_conv = raw[valid - k1:valid, :]
        convbuf[8 - k1:8, :] = raw[T - k1:T, :]
        convn_ref[0] = new_conv
        xbc = _silu(acc)
        xs = xbc[:, :d_inner]
        bm = xbc[:, d_inner:d_inner + SSD_N_GROUPS * n]
        cm = xbc[:, d_inner + SSD_N_GROUPS * n:]

        dt = _softplus(small_ref[0, pl.ds(r0, T), :][:, :n_heads] + dtb_ref[...])
        dtT = _softplus(dtT_ref[0, c] + dtbc_ref[...])
        if valid is not None:
            dt = jnp.where(_iota2((T, n_heads), 0) < valid, dt, 0.0)
            dtT = jnp.where(_iota2((n_heads, T), 1) < valid, dtT, 0.0)
        a_cs = _dot(lmat, dt * a_row, precision=HIGHEST)
        a_csT = _dot(dtT * a_col, umat, precision=HIGHEST)
        eacs = jnp.exp(a_cs)
        dend = jnp.exp(a_cs[T - 1:T, :] - a_cs)
        expand = lambda x: _dot(jnp.concatenate(_split3(x), axis=1), emat)
        dt_x = expand(dt)
        dend_x = expand(dend)
        eacs_x = expand(eacs)
        xdt = xs * dt_x
        xte = xdt * dend_x

        for g in range(SSD_N_GROUPS):
            bg = cast(bm[:, g * n:(g + 1) * n])
            cg = cast(cm[:, g * n:(g + 1) * n])
            cb = _dot_nt(cg, bg)
            for jp in range(hpg // 2):
                h1 = g * hpg + 2 * jp
                lo = h1 * SSD_HEAD_DIM
                xdt_p = cast(xdt[:, lo:lo + LANES])
                yd = []
                for h in (h1, h1 + 1):
                    seg = a_cs[:, h:h + 1] - a_csT[h:h + 1, :]
                    dec = jnp.exp(jnp.where(causal, seg, -jnp.inf))
                    yd.append(_dot(cast(cb * dec), xdt_p))
                y_diag = jnp.where(lane_lo, yd[0], yd[1])
                st = ssm_ref[0, h1:h1 + 2].reshape(2 * SSD_HEAD_DIM, n)
                y_off = _dot_nt(cg, cast(st)) * eacs_x[:, lo:lo + LANES]
                cs = _dot_tn(cast(xte[:, lo:lo + LANES]), bg)
                dl = jnp.where(row_lo, eacs[T - 1:T, h1:h1 + 1], eacs[T - 1:T, h1 + 1:h1 + 2])
                ssm_ref[0, h1:h1 + 2] = (st * dl + cs).reshape(2, SSD_HEAD_DIM, n)
                ybuf[:, lo:lo + LANES] = (y_diag + y_off) + dskip_ref[:, lo:lo + LANES] * xs[:, lo:lo + LANES]

        zz = z_ref[0, pl.ds(r0, T), :]
        hg = ybuf[...] * _silu(zz)
        for g in range(SSD_N_GROUPS):
            hgg = hg[:, g * gw:(g + 1) * gw]
            y_ref[0, pl.ds(r0, T), g * gw:(g + 1) * gw] = _rms(hgg, gnorm_ref[:, g * gw:(g + 1) * gw]).astype(y_ref.dtype)
        return carry

    lax.fori_loop(0, CPS, chunk, 0)


def _ssd(xbc, z, small, conv0, ssm0, conv_w, conv_b, dt_bias, a_log, d_skip, ssd_norm, *, T, CPS, valid):
    b, l, cdim = xbc.shape
    d_inner = z.shape[-1]
    nh = d_inner // SSD_HEAD_DIM
    rows = T * CPS
    nsteps = l // rows
    assert l % rows == 0 and (valid is None or (nsteps == 1 and CPS == 1 and valid >= SSD_CONV - 1))
    dtT = jnp.swapaxes(small[:, :, :nh].reshape(b, l // T, T, nh), 2, 3)
    emat = jnp.asarray(np.tile(np.repeat(np.eye(nh, dtype=np.float32), SSD_HEAD_DIM, axis=1), (3, 1)))
    dskip_x = jnp.repeat(d_skip.astype(F32), SSD_HEAD_DIM).reshape(1, d_inner)
    aligned = T % 16 == 0
    cast = (lambda v: v.astype(BF16)) if aligned else (lambda v: v)
    full = lambda shape: pl.BlockSpec(shape, lambda i, s: (0,) * len(shape))
    y, convn, ssmn = pl.pallas_call(
        functools.partial(_ssd_kernel, T=T, CPS=CPS, valid=valid, n_heads=nh, cast=cast),
        grid=(b, nsteps),
        in_specs=[
            pl.BlockSpec((1, rows, cdim), lambda i, s: (i, s, 0)),
            pl.BlockSpec((1, rows, d_inner), lambda i, s: (i, s, 0)),
            pl.BlockSpec((1, rows, LANES), lambda i, s: (i, s, 0)),
            pl.BlockSpec((1, CPS, nh, T), lambda i, s: (i, s, 0, 0)),
            pl.BlockSpec((1, SSD_CONV - 1, cdim), lambda i, s: (i, 0, 0)),
            pl.BlockSpec((1, nh, SSD_HEAD_DIM, SSD_D_STATE), lambda i, s: (i, 0, 0, 0)),
            full((SSD_CONV, cdim)), full((1, cdim)), full((1, nh)), full((nh, 1)), full((1, nh)), full((nh, 1)),
            full((1, d_inner)), full((1, d_inner)), full((3 * nh, d_inner)),
        ],
        out_specs=[
            pl.BlockSpec((1, rows, d_inner), lambda i, s: (i, s, 0)),
            pl.BlockSpec((1, SSD_CONV - 1, cdim), lambda i, s: (i, 0, 0)),
            pl.BlockSpec((1, nh, SSD_HEAD_DIM, SSD_D_STATE), lambda i, s: (i, 0, 0, 0)),
        ],
        out_shape=[
            jax.ShapeDtypeStruct((b, l, d_inner), BF16 if aligned else F32),
            jax.ShapeDtypeStruct((b, SSD_CONV - 1, cdim), F32),
            jax.ShapeDtypeStruct((b, nh, SSD_HEAD_DIM, SSD_D_STATE), F32),
        ],
        scratch_shapes=[pltpu.VMEM((8 + T, cdim), F32), pltpu.VMEM((T, d_inner), F32)],
        compiler_params=_params("parallel", "arbitrary"),
        name="ssd",
    )(xbc, z, small, dtT, conv0, ssm0, conv_w, conv_b.reshape(1, cdim), dt_bias.reshape(1, nh), dt_bias.reshape(nh, 1),
      a_log.reshape(1, nh), a_log.reshape(nh, 1), dskip_x, ssd_norm.reshape(1, d_inner), emat)
    return y, convn, ssmn


_AUG = 3


def _fox_gate_kernel(small_ref, q_ref, k_ref, bf_ref, p_ref, c_ref, logf_ref, qa_ref, ka_ref, carry, *, tr, f_off):
    nh = FOX_N_HEADS

    @pl.when(pl.program_id(1) == 0)
    def _():
        carry[...] = jnp.zeros_like(carry)

    logf = -_softplus(-(small_ref[0][:, f_off:f_off + nh] + bf_ref[...]))
    logf_ref[0] = logf
    lmat = (_iota2((tr, tr), 0) >= _iota2((tr, tr), 1)).astype(F32)
    fcum = _dot(jnp.concatenate([lmat] * 3, axis=1), jnp.concatenate(_split3(logf), axis=0)) + carry[...]
    carry[...] = fcum[tr - 1:tr, :]
    x = _dot(jnp.concatenate(_split3(fcum * LOG2E), axis=1), p_ref[...]) + c_ref[...]
    xq = x[:, :nh * LANES]
    xk = x[:, nh * LANES:]
    lane = _iota2((tr, LANES), 1)
    for h in range(nh):
        sel = (lane < FOX_HEAD_DIM) if h % 2 == 0 else (lane >= FOX_HEAD_DIM)
        pair = slice((h // 2) * LANES, (h // 2 + 1) * LANES)
        blk = slice(h * LANES, (h + 1) * LANES)
        qa_ref[0, h] = jnp.where(sel, q_ref[0][:, pair].astype(F32), xq[:, blk]).astype(BF16)
        ka_ref[0, h] = jnp.where(sel, k_ref[0][:, pair].astype(F32), xk[:, blk]).astype(BF16)


def _aug_tables():
    nh = FOX_N_HEADS
    place = np.zeros((_AUG, nh, 2, nh * LANES), np.float32)
    const = np.zeros((1, 2, nh * LANES), np.float32)
    for h in range(nh):
        off = h * LANES + (FOX_HEAD_DIM if h % 2 == 0 else 0)
        for i in range(_AUG):
            place[i, h, 0, off + i] = 1.0
            const[0, 1, off + i] = 1.0
            const[0, 0, off + _AUG + i] = 1.0
            place[i, h, 1, off + _AUG + i] = -1.0
    return jnp.asarray(place.reshape(_AUG * nh, 2 * nh * LANES)), jnp.asarray(const.reshape(1, 2 * nh * LANES))


def _fox_gate(small, qkv, b_fox_f, *, tr, f_off):
    b, l, _ = small.shape
    nh = FOX_N_HEADS
    place, const = _aug_tables()
    full = lambda shape: pl.BlockSpec(shape, lambda i, s: (0,) * len(shape))
    return pl.pallas_call(
        functools.partial(_fox_gate_kernel, tr=tr, f_off=f_off),
        grid=(b, l // tr),
        in_specs=[
            pl.BlockSpec((1, tr, LANES), lambda i, s: (i, s, 0)),
            pl.BlockSpec((1, tr, nh * FOX_HEAD_DIM), lambda i, s: (i, s, 0)),
            pl.BlockSpec((1, tr, nh * FOX_HEAD_DIM), lambda i, s: (i, s, 1)),
            full((1, nh)), full((_AUG * nh, 2 * nh * LANES)), full((1, 2 * nh * LANES)),
        ],
        out_specs=[
            pl.BlockSpec((1, tr, nh), lambda i, s: (i, s, 0)),
            pl.BlockSpec((1, nh, tr, LANES), lambda i, s: (i, 0, s, 0)),
            pl.BlockSpec((1, nh, tr, LANES), lambda i, s: (i, 0, s, 0)),
        ],
        out_shape=[
            jax.ShapeDtypeStruct((b, l, nh), F32),
            jax.ShapeDtypeStruct((b, nh, l, LANES), BF16),
            jax.ShapeDtypeStruct((b, nh, l, LANES), BF16),
        ],
        scratch_shapes=[pltpu.VMEM((1, nh), F32)],
        compiler_params=_params("parallel", "arbitrary"),
        name="fox_gate",
    )(small, qkv, qkv, b_fox_f.reshape(1, nh), place, const)


def _fox_prompt_kernel(qi_ref, ki_ref, q_ref, k_ref, v_ref, o_ref, m_scr, acc_scr, *, tq, hps):
    s_idx = pl.program_id(2)
    qi = qi_ref[s_idx]
    ki = ki_ref[s_idx]

    @pl.when(ki == 0)
    def _():
        m_scr[...] = jnp.full_like(m_scr, -jnp.inf)
        acc_scr[...] = jnp.zeros_like(acc_scr)

    def body(masked):
        reps = tq // LANES
        lane_lo = _iota2((tq, LANES), 1) < FOX_HEAD_DIM
        if masked:
            keep = _iota2((tq, tq), 1) <= _iota2((tq, tq), 0)
        for hh in range(hps):
            pair = slice((hh // 2) * LANES, (hh // 2 + 1) * LANES)
            own = lane_lo if hh % 2 == 0 else jnp.logical_not(lane_lo)
            v1 = jnp.where(own, v_ref[0, :, pair], 1.0).astype(BF16)
            s = _dot_nt(q_ref[0, hh], k_ref[0, hh])
            if masked:
                s = jnp.where(keep, s, -jnp.inf)
            m_prev = m_scr[hh]
            m_new = jnp.maximum(m_prev, jnp.max(s, axis=1, keepdims=True))
            p = jnp.exp2(s - jnp.tile(m_new, (1, reps)))
            acc_scr[hh] = jnp.exp2(m_prev - m_new) * acc_scr[hh] + _dot(p.astype(BF16), v1)
            m_scr[hh] = m_new

    pl.when(ki < qi)(lambda: body(False))

    @pl.when(ki == qi)
    def _():
        body(True)
        lane_lo = _iota2((tq, LANES), 1) < FOX_HEAD_DIM
        for pp in range(hps // 2):
            a0, a1 = acc_scr[2 * pp], acc_scr[2 * pp + 1]
            o0 = a0 / pltpu.roll(a0, FOX_HEAD_DIM, axis=1)
            o1 = a1 / pltpu.roll(a1, FOX_HEAD_DIM, axis=1)
            o_ref[0, :, pp * LANES:(pp + 1) * LANES] = jnp.where(lane_lo, o0, o1).astype(o_ref.dtype)


def _fox_prompt(q_aug, k_aug, v, *, tq, hps, v_col=0):
    b, nh, l, _ = q_aug.shape
    nq = l // tq
    vw = hps * FOX_HEAD_DIM
    v0 = v_col // vw
    qi = np.array([i for i in range(nq) for _ in range(i + 1)], np.int32)
    ki = np.array([j for i in range(nq) for j in range(i + 1)], np.int32)
    grid_spec = pltpu.PrefetchScalarGridSpec(
        num_scalar_prefetch=2,
        grid=(b, nh // hps, len(qi)),
        in_specs=[
            pl.BlockSpec((1, hps, tq, LANES), lambda i, p, s, qi, ki: (i, p, qi[s], 0)),
            pl.BlockSpec((1, hps, tq, LANES), lambda i, p, s, qi, ki: (i, p, ki[s], 0)),
            pl.BlockSpec((1, tq, vw), lambda i, p, s, qi, ki: (i, ki[s], v0 + p)),
        ],
        out_specs=pl.BlockSpec((1, tq, vw), lambda i, p, s, qi, ki: (i, qi[s], p)),
        scratch_shapes=[pltpu.VMEM((hps, tq, LANES), F32), pltpu.VMEM((hps, tq, LANES), F32)],
    )
    return pl.pallas_call(
        functools.partial(_fox_prompt_kernel, tq=tq, hps=hps),
        grid_spec=grid_spec,
        out_shape=jax.ShapeDtypeStruct((b, l, nh * FOX_HEAD_DIM), BF16),
        compiler_params=_params("parallel", "parallel", "arbitrary"),
        name="fox_prompt",
    )(jnp.asarray(qi), jnp.asarray(ki), q_aug, k_aug, v)


def _gate_scan_kernel(lf_ref, u_ref, o_ref):
    o_ref[...] = _dot(jnp.concatenate(_split3(lf_ref[...]), axis=1), u_ref[...])


def _gate_scan(lf_rows, *, rows):
    n, page = lf_rows.shape
    later = np.triu(np.ones((page, page), np.float32), 1).T
    u3 = jnp.asarray(np.tile(np.concatenate([later, np.ones((page, page), np.float32)], axis=1), (3, 1)))
    return pl.pallas_call(
        _gate_scan_kernel,
        grid=(n // rows,),
        in_specs=[pl.BlockSpec((rows, page), lambda i: (i, 0)), pl.BlockSpec((3 * page, 2 * page), lambda i: (0, 0))],
        out_specs=pl.BlockSpec((rows, 2 * page), lambda i: (i, 0)),
        out_shape=jax.ShapeDtypeStruct((n, 2 * page), F32),
        compiler_params=_params("parallel"),
        name="gate_scan",
    )(lf_rows, u3)


def _fox_sample_kernel(pt_ref, q_ref, kn_ref, vn_ref, small_ref, bf_ref, *rest, PG, page, nq, f_off):
    kt_refs = rest[:PG]
    vt_refs = rest[PG:2 * PG]
    gl_refs = rest[2 * PG:3 * PG]
    o_ref, logf_ref, qx_scr, m_scr, l_scr, acc_scr, carry_scr = rest[3 * PG:]
    nh, dh = FOX_N_HEADS, FOX_HEAD_DIM
    rows = nq * nh
    width = nh * dh
    g = pl.program_id(1)
    own = (_iota2((rows, width), 1) // dh) == (_iota2((rows, width), 0) % nh)

    logf_new = -_softplus(-(small_ref[0][:, f_off:f_off + nh] + bf_ref[...]))
    lf8 = jnp.where(_iota2((8, nh), 0) < nq, logf_new, 0.0)
    tri8 = (_iota2((8, 8), 0) <= _iota2((8, 8), 1)).astype(F32)
    fnew = _dot_tn(tri8, lf8, precision=HIGHEST)
    fnewT = _dot_tn(lf8, tri8, precision=HIGHEST)
    frow = jnp.concatenate([jnp.broadcast_to(fnew[qq:qq + 1, :], (nh, nh)) for qq in range(nq)], axis=0)
    fcol = jnp.sum(jnp.where(_iota2((rows, nh), 1) == _iota2((rows, nh), 0) % nh, frow, 0.0), axis=1, keepdims=True)

    @pl.when(g == 0)
    def _():
        qrows = jnp.concatenate([jnp.broadcast_to(q_ref[0][qq:qq + 1, :], (nh, width)) for qq in range(nq)], axis=0)
        qx_scr[...] = jnp.where(own, qrows, 0.0)
        m_scr[...] = jnp.full_like(m_scr, -jnp.inf)
        l_scr[...] = jnp.zeros_like(l_scr)
        acc_scr[...] = jnp.zeros_like(acc_scr)
        carry_scr[...] = jnp.zeros_like(carry_scr)
        logf_ref[0] = logf_new

    qx = qx_scr[...]

    scores = []
    carry = carry_scr[...]
    for i in range(PG):
        gl = gl_refs[i][...]
        gate = carry + gl[:, :page]
        carry = carry + gl[:, page:]
        scores.append(_dot(qx, kt_refs[i][0]) + fcol + jnp.concatenate([gate] * nq, axis=0))
    carry_scr[...] = carry
    m_prev = m_scr[...]
    m_new = m_prev
    for s in scores:
        m_new = jnp.maximum(m_new, jnp.max(s, axis=1, keepdims=True))
    alpha = jnp.exp(m_prev - m_new)
    l_new = alpha * l_scr[...]
    acc = alpha * acc_scr[...]
    for i, s in enumerate(scores):
        p = jnp.exp(s - m_new)
        l_new = l_new + jnp.sum(p, axis=1, keepdims=True)
        acc = acc + _dot_nt(p, vt_refs[i][0])
    m_scr[...] = m_new
    l_scr[...] = l_new
    acc_scr[...] = acc

    @pl.when(g == pl.num_programs(1) - 1)
    def _():
        s = _dot_nt(qx, kn_ref[0])
        s = s + fcol - jnp.concatenate([fnewT] * nq, axis=0)
        keep = _iota2((rows, 8), 1) <= _iota2((rows, 8), 0) // nh
        s = jnp.where(keep, s, -jnp.inf)
        m_fin = jnp.maximum(m_scr[...], jnp.max(s, axis=1, keepdims=True))
        a_fin = jnp.exp(m_scr[...] - m_fin)
        p = jnp.exp(s - m_fin)
        l_fin = a_fin * l_scr[...] + jnp.sum(p, axis=1, keepdims=True)
        out = jnp.where(own, (a_fin * acc_scr[...] + _dot(p, vn_ref[0])) / l_fin, 0.0)
        pick = (_iota2((8, rows), 1) // nh == _iota2((8, rows), 0)).astype(F32)
        o_ref[0] = _dot(pick, out, precision=HIGHEST).astype(o_ref.dtype)


def _fox_sample(q, k_new, v_new, small, b_fox_f, cache_kt, cache_vt, gates, page_table, *, nq, PG, f_off):
    b, _, width = q.shape
    nh = FOX_N_HEADS
    n_pages = page_table.shape[1]
    page = cache_kt.shape[2]
    assert n_pages % PG == 0
    rows = nq * nh

    def pmap(i):
        return lambda bb, g, pt: (pt[bb, n_pages - 1 - (g * PG + i)], 0, 0)

    def gmap(i):
        return lambda bb, g, pt: (pt[bb, n_pages - 1 - (g * PG + i)], 0)

    fixed = lambda shape: pl.BlockSpec(shape, lambda bb, g, pt: (bb,) + (0,) * (len(shape) - 1))
    in_specs = [fixed((1, 8, width)), fixed((1, 8, width)), fixed((1, 8, width)), fixed((1, 8, LANES)),
                pl.BlockSpec((1, nh), lambda bb, g, pt: (0, 0))]
    in_specs += [pl.BlockSpec((1, width, page), pmap(i)) for i in range(PG)]
    in_specs += [pl.BlockSpec((1, width, page), pmap(i)) for i in range(PG)]
    in_specs += [pl.BlockSpec((nh, 2 * page), gmap(i)) for i in range(PG)]
    grid_spec = pltpu.PrefetchScalarGridSpec(
        num_scalar_prefetch=1,
        grid=(b, n_pages // PG),
        in_specs=in_specs,
        out_specs=[fixed((1, 8, width)), fixed((1, 8, nh))],
        scratch_shapes=[pltpu.VMEM((rows, width), F32), pltpu.VMEM((rows, 1), F32), pltpu.VMEM((rows, 1), F32),
                        pltpu.VMEM((rows, width), F32), pltpu.VMEM((nh, page), F32)],
    )
    return pl.pallas_call(
        functools.partial(_fox_sample_kernel, PG=PG, page=page, nq=nq, f_off=f_off),
        grid_spec=grid_spec,
        out_shape=[jax.ShapeDtypeStruct((b, 8, width), F32), jax.ShapeDtypeStruct((b, 8, nh), F32)],
        compiler_params=_params("parallel", "arbitrary"),
        name="fox_sample",
    )(page_table, q, k_new, v_new, small, b_fox_f.reshape(1, nh),
      *([cache_kt] * PG), *([cache_vt] * PG), *([gates] * PG))


def _mix_kernel(x_ref, ys_ref, yf_ref, gt_ref, bg_ref, wso_ref, wfo_ref, wmx_ref, nm_ref, wq_ref, x1_ref, qm_ref, *, qscale):
    d = x_ref.shape[-1]
    gate = _sigmoid(gt_ref[...] + bg_ref[...])
    merged = gate[:, :d] * _dot(ys_ref[...], wso_ref[...]) + gate[:, d:] * _dot(yf_ref[...], wfo_ref[...])
    x1 = x_ref[...] + _dot(merged.astype(BF16), wmx_ref[...])
    x1_ref[...] = x1
    h = _rms(x1, nm_ref[...]).astype(BF16)
    qm_ref[...] = (_dot(h, wq_ref[...]) * qscale).astype(qm_ref.dtype)


def _mix(x, y_ssd, y_fox, gates, b_gate, w_so, w_fo, w_mx, norm_mem, w_q, *, tm, qscale, q_dtype):
    m, d = x.shape
    row = lambda w: pl.BlockSpec((tm, w), lambda i: (i, 0))
    full = lambda a: pl.BlockSpec(a.shape, lambda i: (0,) * a.ndim)
    bg, nm = b_gate.reshape(1, -1), norm_mem.reshape(1, -1)
    return pl.pallas_call(
        functools.partial(_mix_kernel, qscale=qscale),
        grid=(m // tm,),
        in_specs=[row(d), row(y_ssd.shape[1]), row(d), row(2 * d), full(bg), full(w_so), full(w_fo), full(w_mx), full(nm), full(w_q)],
        out_specs=[row(d), row(d)],
        out_shape=[jax.ShapeDtypeStruct((m, d), F32), jax.ShapeDtypeStruct((m, d), q_dtype)],
        compiler_params=_params("parallel"),
        name="mix",
    )(x, y_ssd, y_fox, gates, bg, w_so, w_fo, w_mx, nm, w_q)


def _mem_attn_kernel(q_ref, k_ref, v_ref, o_ref, *, cast):
    dh = q_ref.shape[-1] // MEM_N_HEADS
    by_head = len(k_ref.shape) == 4
    for h in range(MEM_N_HEADS):
        cols = slice(h * dh, (h + 1) * dh)
        k = k_ref[0, :, h, :] if by_head else k_ref[0][:, cols]
        v = v_ref[0, :, h, :] if by_head else v_ref[0][:, cols]
        s = _dot_nt(cast(q_ref[0][:, cols]), cast(k))
        p = jnp.exp(s - jnp.max(s, axis=1, keepdims=True))
        p = p / jnp.sum(p, axis=1, keepdims=True)
        o_ref[0, :, cols] = _dot(cast(p), cast(v)).astype(o_ref.dtype)


def _mem_attn(q, mem_k, mem_v, *, tl):
    b, l, d = q.shape
    cast = (lambda v: v.astype(BF16)) if q.dtype == BF16 else (lambda v: v)
    mem_spec = pl.BlockSpec((1,) + mem_k.shape[1:], lambda i, s: (i,) + (0,) * (mem_k.ndim - 1))
    return pl.pallas_call(
        functools.partial(_mem_attn_kernel, cast=cast),
        grid=(b, l // tl),
        in_specs=[pl.BlockSpec((1, tl, d), lambda i, s: (i, s, 0)), mem_spec, mem_spec],
        out_specs=pl.BlockSpec((1, tl, d), lambda i, s: (i, s, 0)),
        out_shape=jax.ShapeDtypeStruct((b, l, d), q.dtype),
        compiler_params=_params("parallel", "parallel"),
        name="mem_attn",
    )(q, mem_k, mem_v)


def _route_kernel(x1_ref, att_ref, wo_ref, nf_ref, wr_ref, br_ref, x2_ref, h_ref, rt_ref):
    x2 = x1_ref[...] + _dot(att_ref[...], wo_ref[...])
    x2_ref[...] = x2
    h = _rms(x2, nf_ref[...])
    h_ref[...] = h.astype(h_ref.dtype)
    h_hi = h.astype(BF16)
    h_lo = (h - h_hi.astype(F32)).astype(BF16)
    logits = _dot(jnp.concatenate([h_hi, h_hi, h_lo], axis=1), wr_ref[...]) + br_ref[...]
    lane = _iota2(logits.shape, 1).astype(F32)
    first = lambda mask: jnp.min(jnp.where(mask, lane, float(LANES)), axis=1, keepdims=True)
    gl = jnp.where(lane < N_EXPERT_GROUPS, logits, -jnp.inf)
    gmax = jnp.max(gl, axis=1, keepdims=True)
    g_idx = first(gl == gmax)
    g_w = 1.0 / jnp.sum(jnp.exp(gl - gmax), axis=1, keepdims=True)
    e_lo = N_EXPERT_GROUPS + g_idx * EXPERTS_PER_GROUP
    el = jnp.where((lane >= e_lo) & (lane < e_lo + EXPERTS_PER_GROUP), logits, -jnp.inf)
    v1 = jnp.max(el, axis=1, keepdims=True)
    i1 = first(el == v1)
    el2 = jnp.where(lane == i1, -jnp.inf, el)
    v2 = jnp.max(el2, axis=1, keepdims=True)
    i2 = first(el2 == v2)
    e21 = jnp.exp(v2 - v1)
    w1 = g_w / (1.0 + e21)
    w2 = g_w * e21 / (1.0 + e21)
    rt_ref[...] = jnp.where(lane == 0, i1 - N_EXPERT_GROUPS, jnp.where(lane == 1, i2 - N_EXPERT_GROUPS,
                            jnp.where(lane == 2, w1, jnp.where(lane == 3, w2, 0.0))))


def _route(x1, att, w_o, norm_ffn, w_router, b_router, *, tm):
    m, d = x1.shape
    row = lambda w: pl.BlockSpec((tm, w), lambda i: (i, 0))
    full = lambda a: pl.BlockSpec(a.shape, lambda i: (0,) * a.ndim)
    nf = norm_ffn.reshape(1, d)
    return pl.pallas_call(
        _route_kernel,
        grid=(m // tm,),
        in_specs=[row(d), row(d), full(w_o), full(nf), full(w_router), full(b_router)],
        out_specs=[row(d), row(d), row(LANES)],
        out_shape=[jax.ShapeDtypeStruct((m, d), F32), jax.ShapeDtypeStruct((m, d), F32), jax.ShapeDtypeStruct((m, LANES), F32)],
        compiler_params=_params("parallel"),
        name="route",
    )(x1, att, w_o, nf, w_router, b_router)


def _expert_kernel(te_ref, nt_ref, x_ref, wg_ref, wu_ref, wd_ref, o_ref):
    @pl.when(pl.program_id(0) < nt_ref[0])
    def _():
        x = x_ref[...].astype(BF16)
        hid = _silu(_dot(x, wg_ref[0].astype(BF16))) * _dot(x, wu_ref[0].astype(BF16))
        o_ref[...] = _dot(hid.astype(BF16), wd_ref[0].astype(BF16)).astype(o_ref.dtype)

    @pl.when(pl.program_id(0) >= nt_ref[0])
    def _():
        o_ref[...] = jnp.zeros_like(o_ref)


def _experts(x_sorted, tile_expert, n_tiles, w_gate, w_up, w_down, *, tm):
    p, d = x_sorted.shape
    ne, _, ff = w_gate.shape
    grid_spec = pltpu.PrefetchScalarGridSpec(
        num_scalar_prefetch=2,
        grid=(p // tm,),
        in_specs=[pl.BlockSpec((tm, d), lambda i, te, nt: (i, 0)),
                  pl.BlockSpec((1, d, ff), lambda i, te, nt: (te[i], 0, 0)),
                  pl.BlockSpec((1, d, ff), lambda i, te, nt: (te[i], 0, 0)),
                  pl.BlockSpec((1, ff, d), lambda i, te, nt: (te[i], 0, 0))],
        out_specs=pl.BlockSpec((tm, d), lambda i, te, nt: (i, 0)),
    )
    return pl.pallas_call(
        _expert_kernel,
        grid_spec=grid_spec,
        out_shape=jax.ShapeDtypeStruct((p, d), F32),
        compiler_params=_params("arbitrary"),
        name="experts",
    )(tile_expert, n_tiles, x_sorted, w_gate, w_up, w_down)


def _experts_dense_kernel(h_ref, rt_ref, wg_ref, wu_ref, wd_ref, o_ref):
    e = pl.program_id(0)

    @pl.when(e == 0)
    def _():
        o_ref[...] = jnp.zeros_like(o_ref)

    rt = rt_ref[...]
    ef = e.astype(F32)
    cw = jnp.where(rt[:, 0:1] == ef, rt[:, 2:3], 0.0) + jnp.where(rt[:, 1:2] == ef, rt[:, 3:4], 0.0)
    x = h_ref[...].astype(BF16)
    hid = _silu(_dot(x, wg_ref[0].astype(BF16))) * _dot(x, wu_ref[0].astype(BF16))
    o_ref[...] = o_ref[...] + cw * _dot(hid.astype(BF16), wd_ref[0].astype(BF16))


def _experts_dense(h, rt, w_gate, w_up, w_down):
    m, d = h.shape
    ne, _, ff = w_gate.shape
    return pl.pallas_call(
        _experts_dense_kernel,
        grid=(ne,),
        in_specs=[pl.BlockSpec((m, d), lambda e: (0, 0)), pl.BlockSpec((m, LANES), lambda e: (0, 0)),
                  pl.BlockSpec((1, d, ff), lambda e: (e, 0, 0)), pl.BlockSpec((1, d, ff), lambda e: (e, 0, 0)),
                  pl.BlockSpec((1, ff, d), lambda e: (e, 0, 0))],
        out_specs=pl.BlockSpec((m, d), lambda e: (0, 0)),
        out_shape=jax.ShapeDtypeStruct((m, d), F32),
        compiler_params=_params("arbitrary"),
        name="experts_dense",
    )(h, rt, w_gate, w_up, w_down)


def _final_kernel(x2_ref, ya_ref, yb_ref, rt_ref, g_ref, o_ref):
    rt = rt_ref[...]
    moe = rt[:, 2:3] * ya_ref[...] + rt[:, 3:4] * yb_ref[...]
    o_ref[...] = _rms(x2_ref[...] + moe, g_ref[...])


def _final(x2, y2, rt, norm_final, *, tm):
    m, d = x2.shape
    row = lambda w: pl.BlockSpec((tm, w), lambda i: (i, 0))
    slot = lambda k: pl.BlockSpec((None, tm, d), lambda i: (k, i, 0))
    return pl.pallas_call(
        _final_kernel,
        grid=(m // tm,),
        in_specs=[row(d), slot(0), slot(1), row(LANES), pl.BlockSpec((1, d), lambda i: (0, 0))],
        out_specs=row(d),
        out_shape=jax.ShapeDtypeStruct((m, d), F32),
        compiler_params=_params("parallel"),
        name="final",
    )(x2, y2, y2, rt, norm_final.reshape(1, d))


def _final_dense_kernel(x2_ref, moe_ref, g_ref, o_ref):
    o_ref[...] = _rms(x2_ref[...] + moe_ref[...], g_ref[...])


def _final_dense(x2, moe, norm_final):
    m, d = x2.shape
    return pl.pallas_call(
        _final_dense_kernel,
        out_shape=jax.ShapeDtypeStruct((m, d), F32),
        name="final_dense",
    )(x2, moe, norm_final.reshape(1, d))


def _dispatch_plan(eid, tm):
    n = eid.shape[0]
    p = (n + N_EXPERTS * (tm - 1) + tm - 1) // tm * tm
    onehot = (eid[:, None] == jnp.arange(N_EXPERTS, dtype=jnp.int32)[None, :]).astype(jnp.int32)
    csum = jnp.cumsum(onehot, axis=0)
    counts = csum[-1]
    rank = jnp.sum(csum * onehot, axis=1) - 1
    padded = (counts + tm - 1) // tm * tm
    pend = jnp.cumsum(padded)
    pstart = pend - padded
    pos = jnp.sum(pstart[None, :] * onehot, axis=1) + rank
    src = jnp.full((p,), -1, jnp.int32).at[pos].set(jnp.arange(n, dtype=jnp.int32), unique_indices=True)
    n_tiles = (pend[-1] // tm).astype(jnp.int32)
    tile_first = jnp.arange(p // tm, dtype=jnp.int32) * tm
    tile_expert = jnp.sum((tile_first[:, None] >= pend[None, :]).astype(jnp.int32), axis=1)
    last_used = jnp.sum((jnp.maximum(pend[-1] - tm, 0) >= pend).astype(jnp.int32))
    tile_expert = jnp.where(tile_first < pend[-1], tile_expert, last_used).astype(jnp.int32)
    return src, pos, tile_expert, n_tiles.reshape(1)


def _bf(w):
    return w.astype(BF16)


def kernel(x_prompt, x_sample, mem_prompt, cache_fox_k, cache_fox_v, cache_fox_logf, page_table, cache_mem_k, cache_mem_v, state_conv, state_ssm, norm_mix, w_in, conv_w, conv_b, dt_bias, a_log, d_skip, ssd_norm, w_ssd_out, b_fox_f, w_fox_out, b_gate, w_mix_out, norm_mem, norm_mem_kv, w_mem_q, w_mem_k, w_mem_v, w_mem_o, norm_ffn, w_router_group, b_router_group, w_router_expert, b_router_expert, w_exp_gate, w_exp_up, w_exp_down, norm_final):
    depth = w_in.shape[0]
    assert depth == 1, "single-layer step"
    bp, sp, d = x_prompt.shape
    bs, ss, _ = x_sample.shape
    tp, ts = bp * sp, bs * ss
    d_inner = ssd_norm.shape[-1]
    cdim = conv_w.shape[-1]
    nh_ssd = dt_bias.shape[-1]
    nh, dh = FOX_N_HEADS, FOX_HEAD_DIM
    fw = nh * dh
    mem_len = mem_prompt.shape[1]
    l = 0

    o_z, o_x, o_dt, o_q, o_k, o_v, o_f, o_g = np.cumsum([0, d_inner, cdim, nh_ssd, fw, fw, fw, nh]).tolist()
    wi = w_in[l]
    tn = 1024
    seg = lambda a, b: _tile_cols(_bf(wi[:, a:b]), tn)
    w_z, w_x, w_qf, w_kf, w_vf, w_g = seg(o_z, o_x), seg(o_x, o_dt), seg(o_q, o_k), seg(o_k, o_v), seg(o_v, o_f), seg(o_g, o_g + 2 * d)
    n_small = nh_ssd + nh
    w_small = _bf(jnp.pad(jnp.concatenate([wi[:, o_dt:o_q], wi[:, o_f:o_g]], axis=1), ((0, 0), (0, LANES - n_small))))[None]
    f_off = nh_ssd
    w_so, w_fo, w_mx, w_q, w_o = _bf(w_ssd_out[l]), _bf(w_fox_out[l]), _bf(w_mix_out[l]), _bf(w_mem_q[l]), _bf(w_mem_o[l])
    n_r = N_EXPERT_GROUPS + N_EXPERTS
    w_router = jnp.pad(jnp.concatenate([w_router_group[l], w_router_expert[l]], axis=1), ((0, 0), (0, LANES - n_r)))
    wr_hi = _bf(w_router)
    wr_lo = _bf(w_router - wr_hi.astype(F32))
    w_router = jnp.concatenate([wr_hi, wr_lo, wr_hi], axis=0)
    b_router = jnp.pad(jnp.concatenate([b_router_group[l], b_router_expert[l]]), (0, LANES - n_r)).reshape(1, LANES)
    mem_scale = (d // MEM_N_HEADS) ** -0.5

    w_qkv = jnp.concatenate([w_qf, w_kf, w_vf], axis=0)

    def in_proj(x2d, tm, qkv_dtype, q_scale):
        h = _rmsnorm(x2d, norm_mix[l], tm)
        mm = lambda w, dt=F32, sc=1.0: _matmul(h, w, dt, tm=tm, scale=sc)
        return (mm(w_z), mm(w_x), mm(w_qkv, qkv_dtype, q_scale), mm(w_g), mm(w_small)), h

    def post(x2d, y_ssd, y_fox, gates, mem_k, mem_v, nb, tm, tl):
        per = x2d.shape[0] // nb
        q_dtype = BF16 if per % 16 == 0 else F32
        x1, qm = _mix(x2d, y_ssd, y_fox, gates, b_gate[l], w_so, w_fo, w_mx, norm_mem[l], w_q, tm=tm, qscale=mem_scale, q_dtype=q_dtype)
        qm = qm.reshape(nb, per, d)
        if per % 8:
            qm = jnp.pad(qm, ((0, 0), (0, 8 - per % 8), (0, 0)))
        att = _mem_attn(qm, mem_k, mem_v, tl=tl)[:, :per].reshape(-1, d).astype(BF16)
        return _route(x1, att, w_o, norm_ffn[l], w_router, b_router, tm=tm)

    xp = x_prompt.reshape(tp, d)
    (z, xbc, qkv, gates, small), h_in = in_proj(xp, 1024, BF16, dh ** -0.5 * LOG2E)
    qkv = qkv.reshape(bp, sp, 3 * fw)
    k_out = _matmul_t(_bf(wi[:, o_k:o_v]).T[None], h_in, seqs=bp, tm=1024).reshape(bp, nh, dh, sp).transpose(0, 3, 1, 2)
    v_out = _matmul_t(_bf(wi[:, o_v:o_f]).T[None], h_in, seqs=bp, tm=1024).reshape(bp, nh, dh, sp).transpose(0, 3, 1, 2)
    conv0 = jnp.zeros((bp, SSD_CONV - 1, cdim), F32)
    ssm0 = jnp.zeros((bp, nh_ssd, SSD_HEAD_DIM, SSD_D_STATE), F32)
    y_ssd, pconv, pssm = _ssd(xbc.reshape(bp, sp, cdim), z.reshape(bp, sp, d_inner), small.reshape(bp, sp, LANES), conv0, ssm0,
                              conv_w[l], conv_b[l], dt_bias[l], a_log[l], d_skip[l], ssd_norm[l], T=128, CPS=4, valid=None)
    plogf, q_aug, k_aug = _fox_gate(small.reshape(bp, sp, LANES), qkv, b_fox_f[l], tr=512, f_off=f_off)
    y_fox = _fox_prompt(q_aug, k_aug, qkv, tq=1024, hps=4, v_col=2 * fw)
    mem_h = _rmsnorm(mem_prompt.reshape(bp * mem_len, d), norm_mem_kv[l], 256)
    mk = _matmul(mem_h, _tile_cols(_bf(w_mem_k[l]), 512), F32, tm=256)
    mv = _matmul(mem_h, _tile_cols(_bf(w_mem_v[l]), 512), F32, tm=256)
    x2p, hp, rtp = post(xp, y_ssd.reshape(tp, d_inner), y_fox.reshape(tp, fw), gates, mk.reshape(bp, mem_len, d), mv.reshape(bp, mem_len, d), bp, 256, 512)

    tm_e = 256
    eid = rtp[:, :2].astype(jnp.int32).reshape(-1)
    src, pos, tile_expert, n_tiles = _dispatch_plan(eid, tm_e)
    take = lambda a, idx: a.at[idx].get(mode="promise_in_bounds")
    src, x_sample = lax.optimization_barrier((src, x_sample))
    x_sorted = take(hp, jnp.maximum(src, 0) // 2)

    xs = x_sample.reshape(ts, d)
    (zs, xbcs, qkv_s, gates_s, small_s), _ = in_proj(xs, ts, F32, dh ** -0.5)
    qs, ks, vs = qkv_s[:, :fw], qkv_s[:, fw:2 * fw], qkv_s[:, 2 * fw:]
    pad8 = lambda a: jnp.pad(a.reshape(bs, ss, -1), ((0, 0), (0, 8 - ss), (0, 0)))
    ys_ssd, sconv, sssm = _ssd(pad8(xbcs), pad8(zs), pad8(small_s), state_conv[l], state_ssm[l],
                               conv_w[l], conv_b[l], dt_bias[l], a_log[l], d_skip[l], ssd_norm[l], T=8, CPS=1, valid=ss)
    pool, page = cache_fox_k.shape[1], cache_fox_k.shape[2]
    kt = jnp.transpose(cache_fox_k[l], (0, 2, 3, 1)).reshape(pool, fw, page)
    vt = jnp.transpose(cache_fox_v[l], (0, 2, 3, 1)).reshape(pool, fw, page)
    gates_past = _gate_scan(jnp.transpose(cache_fox_logf[l], (0, 2, 1)).reshape(pool * nh, page), rows=2048)
    ys_fox, slogf = _fox_sample(pad8(qs), pad8(ks), pad8(vs), pad8(small_s), b_fox_f[l], kt, vt, gates_past, page_table,
                                nq=ss, PG=16, f_off=f_off)
    x2s, hs, rts = post(xs, ys_ssd[:, :ss].reshape(ts, d_inner).astype(BF16), ys_fox[:, :ss].reshape(ts, fw).astype(BF16), gates_s,
                        cache_mem_k[l], cache_mem_v[l], bs, ts, 8)

    moe_s = _experts_dense(hs, rts, w_exp_gate[l], w_exp_up[l], w_exp_down[l])
    y_sample = _final_dense(x2s, moe_s, norm_final).reshape(bs, ss, d)

    x_sorted, y_sample = lax.optimization_barrier((x_sorted, y_sample))
    y_sorted = _experts(x_sorted, tile_expert, n_tiles, w_exp_gate[l], w_exp_up[l], w_exp_down[l], tm=tm_e)
    y_prompt = _final(x2p, take(y_sorted, pos.reshape(tp, 2).T), rtp, norm_final, tm=512).reshape(bp, sp, d)

    return (y_prompt, y_sample,
            k_out[None], v_out[None], plogf.reshape(1, bp, sp, nh),
            mk.reshape(1, bp, mem_len, MEM_N_HEADS, d // MEM_N_HEADS), mv.reshape(1, bp, mem_len, MEM_N_HEADS, d // MEM_N_HEADS),
            pconv[None], pssm[None],
            ks.reshape(1, bs, ss, nh, dh), vs.reshape(1, bs, ss, nh, dh), slogf[:, :ss][None],
            sconv[None], sssm[None])
```

```python
import functools

import numpy as np
import jax
import jax.numpy as jnp
from jax import lax
from jax.experimental import pallas as pl
from jax.experimental.pallas import tpu as pltpu

F32 = jnp.float32
BF16 = jnp.bfloat16
HIGHEST = lax.Precision.HIGHEST

RMS_EPS = 1e-6
LOG2E = 1.4426950408889634
SSD_HEAD_DIM = 64
SSD_N_GROUPS = 4
SSD_D_STATE = 128
SSD_CONV = 4
FOX_N_HEADS = 16
FOX_HEAD_DIM = 64
MEM_N_HEADS = 4
N_EXPERT_GROUPS = 4
EXPERTS_PER_GROUP = 8
N_EXPERTS = N_EXPERT_GROUPS * EXPERTS_PER_GROUP

LANES = 128
VMEM_LIMIT = 56 * 1024 * 1024


def _params(*sem):
    return pltpu.CompilerParams(dimension_semantics=sem, vmem_limit_bytes=VMEM_LIMIT)


def _dot(a, b, **kw):
    return jnp.dot(a, b, preferred_element_type=F32, **kw)


def _dot_nt(a, b, **kw):
    return lax.dot_general(a, b, (((1,), (1,)), ((), ())), preferred_element_type=F32, **kw)


def _dot_tn(a, b, **kw):
    return lax.dot_general(a, b, (((0,), (0,)), ((), ())), preferred_element_type=F32, **kw)


def _softplus(x):
    return jnp.maximum(x, 0.0) + jnp.log1p(jnp.exp(-jnp.abs(x)))


def _sigmoid(x):
    return 1.0 / (1.0 + jnp.exp(-x))


def _silu(x):
    return x * _sigmoid(x)


def _rms(x, g):
    return x * lax.rsqrt(jnp.mean(x * x, axis=-1, keepdims=True) + RMS_EPS) * g


def _split3(f):
    hi = f.astype(BF16).astype(F32)
    r = f - hi
    mid = r.astype(BF16).astype(F32)
    lo = (r - mid).astype(BF16).astype(F32)
    return hi, mid, lo


def _iota2(shape, dim):
    return lax.broadcasted_iota(jnp.int32, shape, dim)


def _rmsnorm_kernel(x_ref, g_ref, o_ref):
    o_ref[...] = _rms(x_ref[...].astype(F32), g_ref[...]).astype(o_ref.dtype)


def _rmsnorm(x, g, tm):
    m, d = x.shape
    return pl.pallas_call(
        _rmsnorm_kernel,
        grid=(m // tm,),
        in_specs=[pl.BlockSpec((tm, d), lambda i: (i, 0)), pl.BlockSpec((1, d), lambda i: (0, 0))],
        out_specs=pl.BlockSpec((tm, d), lambda i: (i, 0)),
        out_shape=jax.ShapeDtypeStruct((m, d), BF16),
        compiler_params=_params("parallel"),
        name="rmsnorm",
    )(x, g.reshape(1, d))


def _mm_kernel(a_ref, w_ref, o_ref, *, scale):
    acc = _dot(a_ref[...], w_ref[0])
    if scale != 1.0:
        acc = acc * jnp.where(pl.program_id(1) == 0, scale, 1.0)
    o_ref[...] = acc.astype(o_ref.dtype)


def _tile_cols(w, tn):
    k, n = w.shape
    return jnp.swapaxes(w.reshape(k, n // tn, tn), 0, 1)


def _matmul(a, w_tiles, out_dtype, *, tm, scale=1.0):
    m, k = a.shape
    nt, _, tn = w_tiles.shape
    return pl.pallas_call(
        functools.partial(_mm_kernel, scale=scale),
        grid=(m // tm, nt),
        in_specs=[pl.BlockSpec((tm, k), lambda i, j: (i, 0)), pl.BlockSpec((1, k, tn), lambda i, j: (j, 0, 0))],
        out_specs=pl.BlockSpec((tm, tn), lambda i, j: (i, j)),
        out_shape=jax.ShapeDtypeStruct((m, nt * tn), out_dtype),
        compiler_params=_params("parallel", "arbitrary"),
        name="matmul",
    )(a, w_tiles)


def _mm_t_kernel(w_ref, a_ref, o_ref):
    o_ref[0] = _dot_nt(w_ref[0], a_ref[...])


def _matmul_t(w_rows, a, *, seqs, tm):
    m, k = a.shape
    nt, tn, _ = w_rows.shape
    per = m // seqs // tm
    return pl.pallas_call(
        _mm_t_kernel,
        grid=(m // tm, nt),
        in_specs=[pl.BlockSpec((1, tn, k), lambda i, j: (j, 0, 0)), pl.BlockSpec((tm, k), lambda i, j: (i, 0))],
        out_specs=pl.BlockSpec((1, tn, tm), lambda i, j: (i // per, j, i % per)),
        out_shape=jax.ShapeDtypeStruct((seqs, nt * tn, m // seqs), F32),
        compiler_params=_params("parallel", "arbitrary"),
        name="matmul_t",
    )(w_rows, a)


def _ssd_kernel(xbc_ref, z_ref, small_ref, dtT_ref, conv0_ref, ssm0_ref,
                convw_ref, convb_ref, dtb_ref, dtbc_ref, alog_ref, alogc_ref, dskip_ref, gnorm_ref, e_ref,
                y_ref, convn_ref, ssm_ref, convbuf, ybuf, *, T, CPS, valid, n_heads, cast):
    step = pl.program_id(1)
    d_inner = n_heads * SSD_HEAD_DIM
    gw = d_inner // SSD_N_GROUPS
    hpg = n_heads // SSD_N_GROUPS
    n = SSD_D_STATE
    k1 = SSD_CONV - 1

    @pl.when(step == 0)
    def _():
        ssm_ref[...] = ssm0_ref[...]
        convbuf[8 - k1:8, :] = conv0_ref[0]

    row_t = _iota2((T, T), 0)
    col_t = _iota2((T, T), 1)
    causal = row_t >= col_t
    lmat = causal.astype(F32)
    umat = (row_t <= col_t).astype(F32)
    lane_lo = _iota2((T, LANES), 1) < SSD_HEAD_DIM
    row_lo = _iota2((LANES, 1), 0) < SSD_HEAD_DIM
    a_row = -jnp.exp(alog_ref[...])
    a_col = -jnp.exp(alogc_ref[...])
    emat = e_ref[...]

    def chunk(c, carry):
        r0 = pl.multiple_of(c * T, T)
        raw = xbc_ref[0, pl.ds(r0, T), :]
        convbuf[8:8 + T, :] = raw
        acc = convb_ref[...] + convbuf[8 - k1:8 - k1 + T, :] * convw_ref[0:1, :]
        for j in range(1, SSD_CONV):
            acc = acc + convbuf[8 - k1 + j:8 - k1 + j + T, :] * convw_ref[j:j + 1, :]
        if valid is None:
            new_conv = raw[T - k1:T, :]
        else:
            new_conv = raw[valid - k1:valid, :]
        convbuf[8 - k1:8, :] = raw[T - k1:T, :]
        convn_ref[0] = new_conv
        xbc = _silu(acc)
        xs = xbc[:, :d_inner]
        bm = xbc[:, d_inner:d_inner + SSD_N_GROUPS * n]
        cm = xbc[:, d_inner + SSD_N_GROUPS * n:]

        dt = _softplus(small_ref[0, pl.ds(r0, T), :][:, :n_heads] + dtb_ref[...])
        dtT = _softplus(dtT_ref[0, c] + dtbc_ref[...])
        if valid is not None:
            dt = jnp.where(_iota2((T, n_heads), 0) < valid, dt, 0.0)
            dtT = jnp.where(_iota2((n_heads, T), 1) < valid, dtT, 0.0)
        a_cs = _dot(lmat, dt * a_row, precision=HIGHEST)
        a_csT = _dot(dtT * a_col, umat, precision=HIGHEST)
        eacs = jnp.exp(a_cs)
        dend = jnp.exp(a_cs[T - 1:T, :] - a_cs)
        expand = lambda x: _dot(jnp.concatenate(_split3(x), axis=1), emat)
        dt_x = expand(dt)
        dend_x = expand(dend)
        eacs_x = expand(eacs)
        xdt = xs * dt_x
        xte = xdt * dend_x

        for g in range(SSD_N_GROUPS):
            bg = cast(bm[:, g * n:(g + 1) * n])
            cg = cast(cm[:, g * n:(g + 1) * n])
            cb = _dot_nt(cg, bg)
            for jp in range(hpg // 2):
                h1 = g * hpg + 2 * jp
                lo = h1 * SSD_HEAD_DIM
                xdt_p = cast(xdt[:, lo:lo + LANES])
                yd = []
                for h in (h1, h1 + 1):
                    seg = a_cs[:, h:h + 1] - a_csT[h:h + 1, :]
                    dec = jnp.exp(jnp.where(causal, seg, -jnp.inf))
                    yd.append(_dot(cast(cb * dec), xdt_p))
                y_diag = jnp.where(lane_lo, yd[0], yd[1])
                st = ssm_ref[0, h1:h1 + 2].reshape(2 * SSD_HEAD_DIM, n)
                y_off = _dot_nt(cg, cast(st)) * eacs_x[:, lo:lo + LANES]
                cs = _dot_tn(cast(xte[:, lo:lo + LANES]), bg)
                dl = jnp.where(row_lo, eacs[T - 1:T, h1:h1 + 1], eacs[T - 1:T, h1 + 1:h1 + 2])
                ssm_ref[0, h1:h1 + 2] = (st * dl + cs).reshape(2, SSD_HEAD_DIM, n)
                ybuf[:, lo:lo + LANES] = (y_diag + y_off) + dskip_ref[:, lo:lo + LANES] * xs[:, lo:lo + LANES]

        zz = z_ref[0, pl.ds(r0, T), :]
        hg = ybuf[...] * _silu(zz)
        for g in range(SSD_N_GROUPS):
            hgg = hg[:, g * gw:(g + 1) * gw]
            y_ref[0, pl.ds(r0, T), g * gw:(g + 1) * gw] = _rms(hgg, gnorm_ref[:, g * gw:(g + 1) * gw]).astype(y_ref.dtype)
        return carry

    lax.fori_loop(0, CPS, chunk, 0)


def _ssd(xbc, z, small, conv0, ssm0, conv_w, conv_b, dt_bias, a_log, d_skip, ssd_norm, *, T, CPS, valid):
    b, l, cdim = xbc.shape
    d_inner = z.shape[-1]
    nh = d_inner // SSD_HEAD_DIM
    rows = T * CPS
    nsteps = l // rows
    assert l % rows == 0 and (valid is None or (nsteps == 1 and CPS == 1 and valid >= SSD_CONV - 1))
    dtT = jnp.swapaxes(small[:, :, :nh].reshape(b, l // T, T, nh), 2, 3)
    emat = jnp.asarray(np.tile(np.repeat(np.eye(nh, dtype=np.float32), SSD_HEAD_DIM, axis=1), (3, 1)))
    dskip_x = jnp.repeat(d_skip.astype(F32), SSD_HEAD_DIM).reshape(1, d_inner)
    aligned = T % 16 == 0
    cast = (lambda v: v.astype(BF16)) if aligned else (lambda v: v)
    full = lambda shape: pl.BlockSpec(shape, lambda i, s: (0,) * len(shape))
    y, convn, ssmn = pl.pallas_call(
        functools.partial(_ssd_kernel, T=T, CPS=CPS, valid=valid, n_heads=nh, cast=cast),
        grid=(b, nsteps),
        in_specs=[
            pl.BlockSpec((1, rows, cdim), lambda i, s: (i, s, 0)),
            pl.BlockSpec((1, rows, d_inner), lambda i, s: (i, s, 0)),
            pl.BlockSpec((1, rows, LANES), lambda i, s: (i, s, 0)),
            pl.BlockSpec((1, CPS, nh, T), lambda i, s: (i, s, 0, 0)),
            pl.BlockSpec((1, SSD_CONV - 1, cdim), lambda i, s: (i, 0, 0)),
            pl.BlockSpec((1, nh, SSD_HEAD_DIM, SSD_D_STATE), lambda i, s: (i, 0, 0, 0)),
            full((SSD_CONV, cdim)), full((1, cdim)), full((1, nh)), full((nh, 1)), full((1, nh)), full((nh, 1)),
            full((1, d_inner)), full((1, d_inner)), full((3 * nh, d_inner)),
        ],
        out_specs=[
            pl.BlockSpec((1, rows, d_inner), lambda i, s: (i, s, 0)),
            pl.BlockSpec((1, SSD_CONV - 1, cdim), lambda i, s: (i, 0, 0)),
            pl.BlockSpec((1, nh, SSD_HEAD_DIM, SSD_D_STATE), lambda i, s: (i, 0, 0, 0)),
        ],
        out_shape=[
            jax.ShapeDtypeStruct((b, l, d_inner), BF16 if aligned else F32),
            jax.ShapeDtypeStruct((b, SSD_CONV - 1, cdim), F32),
            jax.ShapeDtypeStruct((b, nh, SSD_HEAD_DIM, SSD_D_STATE), F32),
        ],
        scratch_shapes=[pltpu.VMEM((8 + T, cdim), F32), pltpu.VMEM((T, d_inner), F32)],
        compiler_params=_params("parallel", "arbitrary"),
        name="ssd",
    )(xbc, z, small, dtT, conv0, ssm0, conv_w, conv_b.reshape(1, cdim), dt_bias.reshape(1, nh), dt_bias.reshape(nh, 1),
      a_log.reshape(1, nh), a_log.reshape(nh, 1), dskip_x, ssd_norm.reshape(1, d_inner), emat)
    return y, convn, ssmn


_AUG = 3


def _fox_gate_kernel(small_ref, q_ref, k_ref, bf_ref, p_ref, c_ref, logf_ref, qa_ref, ka_ref, carry, *, tr, f_off):
    nh = FOX_N_HEADS

    @pl.when(pl.program_id(1) == 0)
    def _():
        carry[...] = jnp.zeros_like(carry)

    logf = -_softplus(-(small_ref[0][:, f_off:f_off + nh] + bf_ref[...]))
    logf_ref[0] = logf
    lmat = (_iota2((tr, tr), 0) >= _iota2((tr, tr), 1)).astype(F32)
    fcum = _dot(jnp.concatenate([lmat] * 3, axis=1), jnp.concatenate(_split3(logf), axis=0)) + carry[...]
    carry[...] = fcum[tr - 1:tr, :]
    x = _dot(jnp.concatenate(_split3(fcum * LOG2E), axis=1), p_ref[...]) + c_ref[...]
    xq = x[:, :nh * LANES]
    xk = x[:, nh * LANES:]
    lane = _iota2((tr, LANES), 1)
    for h in range(nh):
        sel = (lane < FOX_HEAD_DIM) if h % 2 == 0 else (lane >= FOX_HEAD_DIM)
        pair = slice((h // 2) * LANES, (h // 2 + 1) * LANES)
        blk = slice(h * LANES, (h + 1) * LANES)
        qa_ref[0, h] = jnp.where(sel, q_ref[0][:, pair].astype(F32), xq[:, blk]).astype(BF16)
        ka_ref[0, h] = jnp.where(sel, k_ref[0][:, pair].astype(F32), xk[:, blk]).astype(BF16)


def _aug_tables():
    nh = FOX_N_HEADS
    place = np.zeros((_AUG, nh, 2, nh * LANES), np.float32)
    const = np.zeros((1, 2, nh * LANES), np.float32)
    for h in range(nh):
        off = h * LANES + (FOX_HEAD_DIM if h % 2 == 0 else 0)
        for i in range(_AUG):
            place[i, h, 0, off + i] = 1.0
            const[0, 1, off + i] = 1.0
            const[0, 0, off + _AUG + i] = 1.0
            place[i, h, 1, off + _AUG + i] = -1.0
    return jnp.asarray(place.reshape(_AUG * nh, 2 * nh * LANES)), jnp.asarray(const.reshape(1, 2 * nh * LANES))


def _fox_gate(small, qkv, b_fox_f, *, tr, f_off):
    b, l, _ = small.shape
    nh = FOX_N_HEADS
    place, const = _aug_tables()
    full = lambda shape: pl.BlockSpec(shape, lambda i, s: (0,) * len(shape))
    return pl.pallas_call(
        functools.partial(_fox_gate_kernel, tr=tr, f_off=f_off),
        grid=(b, l // tr),
        in_specs=[
            pl.BlockSpec((1, tr, LANES), lambda i, s: (i, s, 0)),
            pl.BlockSpec((1, tr, nh * FOX_HEAD_DIM), lambda i, s: (i, s, 0)),
            pl.BlockSpec((1, tr, nh * FOX_HEAD_DIM), lambda i, s: (i, s, 1)),
            full((1, nh)), full((_AUG * nh, 2 * nh * LANES)), full((1, 2 * nh * LANES)),
        ],
        out_specs=[
            pl.BlockSpec((1, tr, nh), lambda i, s: (i, s, 0)),
            pl.BlockSpec((1, nh, tr, LANES), lambda i, s: (i, 0, s, 0)),
            pl.BlockSpec((1, nh, tr, LANES), lambda i, s: (i, 0, s, 0)),
        ],
        out_shape=[
            jax.ShapeDtypeStruct((b, l, nh), F32),
            jax.ShapeDtypeStruct((b, nh, l, LANES), BF16),
            jax.ShapeDtypeStruct((b, nh, l, LANES), BF16),
        ],
        scratch_shapes=[pltpu.VMEM((1, nh), F32)],
        compiler_params=_params("parallel", "arbitrary"),
        name="fox_gate",
    )(small, qkv, qkv, b_fox_f.reshape(1, nh), place, const)


def _fox_prompt_kernel(qi_ref, ki_ref, q_ref, k_ref, v_ref, o_ref, m_scr, acc_scr, *, tq, hps):
    s_idx = pl.program_id(2)
    qi = qi_ref[s_idx]
    ki = ki_ref[s_idx]

    @pl.when(ki == 0)
    def _():
        m_scr[...] = jnp.full_like(m_scr, -jnp.inf)
        acc_scr[...] = jnp.zeros_like(acc_scr)

    def body(masked):
        reps = tq // LANES
        lane_lo = _iota2((tq, LANES), 1) < FOX_HEAD_DIM
        if masked:
            keep = _iota2((tq, tq), 1) <= _iota2((tq, tq), 0)
        for hh in range(hps):
            pair = slice((hh // 2) * LANES, (hh // 2 + 1) * LANES)
            own = lane_lo if hh % 2 == 0 else jnp.logical_not(lane_lo)
            v1 = jnp.where(own, v_ref[0, :, pair], 1.0).astype(BF16)
            s = _dot_nt(q_ref[0, hh], k_ref[0, hh])
            if masked:
                s = jnp.where(keep, s, -jnp.inf)
            m_prev = m_scr[hh]
            m_new = jnp.maximum(m_prev, jnp.max(s, axis=1, keepdims=True))
            p = jnp.exp2(s - jnp.tile(m_new, (1, reps)))
            acc_scr[hh] = jnp.exp2(m_prev - m_new) * acc_scr[hh] + _dot(p.astype(BF16), v1)
            m_scr[hh] = m_new

    pl.when(ki < qi)(lambda: body(False))

    @pl.when(ki == qi)
    def _():
        body(True)
        lane_lo = _iota2((tq, LANES), 1) < FOX_HEAD_DIM
        for pp in range(hps // 2):
            a0, a1 = acc_scr[2 * pp], acc_scr[2 * pp + 1]
            o0 = a0 / pltpu.roll(a0, FOX_HEAD_DIM, axis=1)
            o1 = a1 / pltpu.roll(a1, FOX_HEAD_DIM, axis=1)
            o_ref[0, :, pp * LANES:(pp + 1) * LANES] = jnp.where(lane_lo, o0, o1).astype(o_ref.dtype)


def _fox_prompt(q_aug, k_aug, v, *, tq, hps, v_col=0):
    b, nh, l, _ = q_aug.shape
    nq = l // tq
    vw = hps * FOX_HEAD_DIM
    v0 = v_col // vw
    qi = np.array([i for i in range(nq) for _ in range(i + 1)], np.int32)
    ki = np.array([j for i in range(nq) for j in range(i + 1)], np.int32)
    grid_spec = pltpu.PrefetchScalarGridSpec(
        num_scalar_prefetch=2,
        grid=(b, nh // hps, len(qi)),
        in_specs=[
            pl.BlockSpec((1, hps, tq, LANES), lambda i, p, s, qi, ki: (i, p, qi[s], 0)),
            pl.BlockSpec((1, hps, tq, LANES), lambda i, p, s, qi, ki: (i, p, ki[s], 0)),
            pl.BlockSpec((1, tq, vw), lambda i, p, s, qi, ki: (i, ki[s], v0 + p)),
        ],
        out_specs=pl.BlockSpec((1, tq, vw), lambda i, p, s, qi, ki: (i, qi[s], p)),
        scratch_shapes=[pltpu.VMEM((hps, tq, LANES), F32), pltpu.VMEM((hps, tq, LANES), F32)],
    )
    return pl.pallas_call(
        functools.partial(_fox_prompt_kernel, tq=tq, hps=hps),
        grid_spec=grid_spec,
        out_shape=jax.ShapeDtypeStruct((b, l, nh * FOX_HEAD_DIM), BF16),
        compiler_params=_params("parallel", "parallel", "arbitrary"),
        name="fox_prompt",
    )(jnp.asarray(qi), jnp.asarray(ki), q_aug, k_aug, v)


def _gate_scan_kernel(lf_ref, u_ref, o_ref):
    o_ref[...] = _dot(jnp.concatenate(_split3(lf_ref[...]), axis=1), u_ref[...])


def _gate_scan(lf_rows, *, rows):
    n, page = lf_rows.shape
    later = np.triu(np.ones((page, page), np.float32), 1).T
    u3 = jnp.asarray(np.tile(np.concatenate([later, np.ones((page, page), np.float32)], axis=1), (3, 1)))
    return pl.pallas_call(
        _gate_scan_kernel,
        grid=(n // rows,),
        in_specs=[pl.BlockSpec((rows, page), lambda i: (i, 0)), pl.BlockSpec((3 * page, 2 * page), lambda i: (0, 0))],
        out_specs=pl.BlockSpec((rows, 2 * page), lambda i: (i, 0)),
        out_shape=jax.ShapeDtypeStruct((n, 2 * page), F32),
        compiler_params=_params("parallel"),
        name="gate_scan",
    )(lf_rows, u3)


def _fox_sample_kernel(pt_ref, q_ref, kn_ref, vn_ref, small_ref, bf_ref, *rest, PG, page, nq, f_off):
    kt_refs = rest[:PG]
    vt_refs = rest[PG:2 * PG]
    gl_refs = rest[2 * PG:3 * PG]
    o_ref, logf_ref, qx_scr, m_scr, l_scr, acc_scr, carry_scr = rest[3 * PG:]
    nh, dh = FOX_N_HEADS, FOX_HEAD_DIM
    rows = nq * nh
    width = nh * dh
    g = pl.program_id(1)
    own = (_iota2((rows, width), 1) // dh) == (_iota2((rows, width), 0) % nh)

    logf_new = -_softplus(-(small_ref[0][:, f_off:f_off + nh] + bf_ref[...]))
    lf8 = jnp.where(_iota2((8, nh), 0) < nq, logf_new, 0.0)
    tri8 = (_iota2((8, 8), 0) <= _iota2((8, 8), 1)).astype(F32)
    fnew = _dot_tn(tri8, lf8, precision=HIGHEST)
    fnewT = _dot_tn(lf8, tri8, precision=HIGHEST)
    frow = jnp.concatenate([jnp.broadcast_to(fnew[qq:qq + 1, :], (nh, nh)) for qq in range(nq)], axis=0)
    fcol = jnp.sum(jnp.where(_iota2((rows, nh), 1) == _iota2((rows, nh), 0) % nh, frow, 0.0), axis=1, keepdims=True)

    @pl.when(g == 0)
    def _():
        qrows = jnp.concatenate([jnp.broadcast_to(q_ref[0][qq:qq + 1, :], (nh, width)) for qq in range(nq)], axis=0)
        qx_scr[...] = jnp.where(own, qrows, 0.0)
        m_scr[...] = jnp.full_like(m_scr, -jnp.inf)
        l_scr[...] = jnp.zeros_like(l_scr)
        acc_scr[...] = jnp.zeros_like(acc_scr)
        carry_scr[...] = jnp.zeros_like(carry_scr)
        logf_ref[0] = logf_new

    qx = qx_scr[...]

    scores = []
    carry = carry_scr[...]
    for i in range(PG):
        gl = gl_refs[i][...]
        gate = carry + gl[:, :page]
        carry = carry + gl[:, page:]
        scores.append(_dot(qx, kt_refs[i][0]) + fcol + jnp.concatenate([gate] * nq, axis=0))
    carry_scr[...] = carry
    m_prev = m_scr[...]
    m_new = m_prev
    for s in scores:
        m_new = jnp.maximum(m_new, jnp.max(s, axis=1, keepdims=True))
    alpha = jnp.exp(m_prev - m_new)
    l_new = alpha * l_scr[...]
    acc = alpha * acc_scr[...]
    for i, s in enumerate(scores):
        p = jnp.exp(s - m_new)
        l_new = l_new + jnp.sum(p, axis=1, keepdims=True)
        acc = acc + _dot_nt(p, vt_refs[i][0])
    m_scr[...] = m_new
    l_scr[...] = l_new
    acc_scr[...] = acc

    @pl.when(g == pl.num_programs(1) - 1)
    def _():
        s = _dot_nt(qx, kn_ref[0])
        s = s + fcol - jnp.concatenate([fnewT] * nq, axis=0)
        keep = _iota2((rows, 8), 1) <= _iota2((rows, 8), 0) // nh
        s = jnp.where(keep, s, -jnp.inf)
        m_fin = jnp.maximum(m_scr[...], jnp.max(s, axis=1, keepdims=True))
        a_fin = jnp.exp(m_scr[...] - m_fin)
        p = jnp.exp(s - m_fin)
        l_fin = a_fin * l_scr[...] + jnp.sum(p, axis=1, keepdims=True)
        out = jnp.where(own, (a_fin * acc_scr[...] + _dot(p, vn_ref[0])) / l_fin, 0.0)
        pick = (_iota2((8, rows), 1) // nh == _iota2((8, rows), 0)).astype(F32)
        o_ref[0] = _dot(pick, out, precision=HIGHEST).astype(o_ref.dtype)


def _fox_sample(q, k_new, v_new, small, b_fox_f, cache_kt, cache_vt, gates, page_table, *, nq, PG, f_off):
    b, _, width = q.shape
    nh = FOX_N_HEADS
    n_pages = page_table.shape[1]
    page = cache_kt.shape[2]
    assert n_pages % PG == 0
    rows = nq * nh

    def pmap(i):
        return lambda bb, g, pt: (pt[bb, n_pages - 1 - (g * PG + i)], 0, 0)

    def gmap(i):
        return lambda bb, g, pt: (pt[bb, n_pages - 1 - (g * PG + i)], 0)

    fixed = lambda shape: pl.BlockSpec(shape, lambda bb, g, pt: (bb,) + (0,) * (len(shape) - 1))
    in_specs = [fixed((1, 8, width)), fixed((1, 8, width)), fixed((1, 8, width)), fixed((1, 8, LANES)),
                pl.BlockSpec((1, nh), lambda bb, g, pt: (0, 0))]
    in_specs += [pl.BlockSpec((1, width, page), pmap(i)) for i in range(PG)]
    in_specs += [pl.BlockSpec((1, width, page), pmap(i)) for i in range(PG)]
    in_specs += [pl.BlockSpec((nh, 2 * page), gmap(i)) for i in range(PG)]
    grid_spec = pltpu.PrefetchScalarGridSpec(
        num_scalar_prefetch=1,
        grid=(b, n_pages // PG),
        in_specs=in_specs,
        out_specs=[fixed((1, 8, width)), fixed((1, 8, nh))],
        scratch_shapes=[pltpu.VMEM((rows, width), F32), pltpu.VMEM((rows, 1), F32), pltpu.VMEM((rows, 1), F32),
                        pltpu.VMEM((rows, width), F32), pltpu.VMEM((nh, page), F32)],
    )
    return pl.pallas_call(
        functools.partial(_fox_sample_kernel, PG=PG, page=page, nq=nq, f_off=f_off),
        grid_spec=grid_spec,
        out_shape=[jax.ShapeDtypeStruct((b, 8, width), F32), jax.ShapeDtypeStruct((b, 8, nh), F32)],
        compiler_params=_params("parallel", "arbitrary"),
        name="fox_sample",
    )(page_table, q, k_new, v_new, small, b_fox_f.reshape(1, nh),
      *([cache_kt] * PG), *([cache_vt] * PG), *([gates] * PG))


def _mix_kernel(x_ref, ys_ref, yf_ref, gt_ref, bg_ref, wso_ref, wfo_ref, wmx_ref, nm_ref, wq_ref, x1_ref, qm_ref, *, qscale):
    d = x_ref.shape[-1]
    gate = _sigmoid(gt_ref[...] + bg_ref[...])
    merged = gate[:, :d] * _dot(ys_ref[...], wso_ref[...]) + gate[:, d:] * _dot(yf_ref[...], wfo_ref[...])
    x1 = x_ref[...] + _dot(merged.astype(BF16), wmx_ref[...])
    x1_ref[...] = x1
    h = _rms(x1, nm_ref[...]).astype(BF16)
    qm_ref[...] = (_dot(h, wq_ref[...]) * qscale).astype(qm_ref.dtype)


def _mix(x, y_ssd, y_fox, gates, b_gate, w_so, w_fo, w_mx, norm_mem, w_q, *, tm, qscale, q_dtype):
    m, d = x.shape
    row = lambda w: pl.BlockSpec((tm, w), lambda i: (i, 0))
    full = lambda a: pl.BlockSpec(a.shape, lambda i: (0,) * a.ndim)
    bg, nm = b_gate.reshape(1, -1), norm_mem.reshape(1, -1)
    return pl.pallas_call(
        functools.partial(_mix_kernel, qscale=qscale),
        grid=(m // tm,),
        in_specs=[row(d), row(y_ssd.shape[1]), row(d), row(2 * d), full(bg), full(w_so), full(w_fo), full(w_mx), full(nm), full(w_q)],
        out_specs=[row(d), row(d)],
        out_shape=[jax.ShapeDtypeStruct((m, d), F32), jax.ShapeDtypeStruct((m, d), q_dtype)],
        compiler_params=_params("parallel"),
        name="mix",
    )(x, y_ssd, y_fox, gates, bg, w_so, w_fo, w_mx, nm, w_q)


def _mem_attn_kernel(q_ref, k_ref, v_ref, o_ref, *, cast):
    dh = q_ref.shape[-1] // MEM_N_HEADS
    by_head = len(k_ref.shape) == 4
    for h in range(MEM_N_HEADS):
        cols = slice(h * dh, (h + 1) * dh)
        k = k_ref[0, :, h, :] if by_head else k_ref[0][:, cols]
        v = v_ref[0, :, h, :] if by_head else v_ref[0][:, cols]
        s = _dot_nt(cast(q_ref[0][:, cols]), cast(k))
        p = jnp.exp(s - jnp.max(s, axis=1, keepdims=True))
        p = p / jnp.sum(p, axis=1, keepdims=True)
        o_ref[0, :, cols] = _dot(cast(p), cast(v)).astype(o_ref.dtype)


def _mem_attn(q, mem_k, mem_v, *, tl):
    b, l, d = q.shape
    cast = (lambda v: v.astype(BF16)) if q.dtype == BF16 else (lambda v: v)
    mem_spec = pl.BlockSpec((1,) + mem_k.shape[1:], lambda i, s: (i,) + (0,) * (mem_k.ndim - 1))
    return pl.pallas_call(
        functools.partial(_mem_attn_kernel, cast=cast),
        grid=(b, l // tl),
        in_specs=[pl.BlockSpec((1, tl, d), lambda i, s: (i, s, 0)), mem_spec, mem_spec],
        out_specs=pl.BlockSpec((1, tl, d), lambda i, s: (i, s, 0)),
        out_shape=jax.ShapeDtypeStruct((b, l, d), q.dtype),
        compiler_params=_params("parallel", "parallel"),
        name="mem_attn",
    )(q, mem_k, mem_v)


def _route_kernel(x1_ref, att_ref, wo_ref, nf_ref, wr_ref, br_ref, x2_ref, h_ref, rt_ref):
    x2 = x1_ref[...] + _dot(att_ref[...], wo_ref[...])
    x2_ref[...] = x2
    h = _rms(x2, nf_ref[...])
    h_ref[...] = h.astype(h_ref.dtype)
    h_hi = h.astype(BF16)
    h_lo = (h - h_hi.astype(F32)).astype(BF16)
    logits = _dot(jnp.concatenate([h_hi, h_hi, h_lo], axis=1), wr_ref[...]) + br_ref[...]
    lane = _iota2(logits.shape, 1).astype(F32)
    first = lambda mask: jnp.min(jnp.where(mask, lane, float(LANES)), axis=1, keepdims=True)
    gl = jnp.where(lane < N_EXPERT_GROUPS, logits, -jnp.inf)
    gmax = jnp.max(gl, axis=1, keepdims=True)
    g_idx = first(gl == gmax)
    g_w = 1.0 / jnp.sum(jnp.exp(gl - gmax), axis=1, keepdims=True)
    e_lo = N_EXPERT_GROUPS + g_idx * EXPERTS_PER_GROUP
    el = jnp.where((lane >= e_lo) & (lane < e_lo + EXPERTS_PER_GROUP), logits, -jnp.inf)
    v1 = jnp.max(el, axis=1, keepdims=True)
    i1 = first(el == v1)
    el2 = jnp.where(lane == i1, -jnp.inf, el)
    v2 = jnp.max(el2, axis=1, keepdims=True)
    i2 = first(el2 == v2)
    e21 = jnp.exp(v2 - v1)
    w1 = g_w / (1.0 + e21)
    w2 = g_w * e21 / (1.0 + e21)
    rt_ref[...] = jnp.where(lane == 0, i1 - N_EXPERT_GROUPS, jnp.where(lane == 1, i2 - N_EXPERT_GROUPS,
                            jnp.where(lane == 2, w1, jnp.where(lane == 3, w2, 0.0))))


def _route(x1, att, w_o, norm_ffn, w_router, b_router, *, tm):
    m, d = x1.shape
    row = lambda w: pl.BlockSpec((tm, w), lambda i: (i, 0))
    full = lambda a: pl.BlockSpec(a.shape, lambda i: (0,) * a.ndim)
    nf = norm_ffn.reshape(1, d)
    return pl.pallas_call(
        _route_kernel,
        grid=(m // tm,),
        in_specs=[row(d), row(d), full(w_o), full(nf), full(w_router), full(b_router)],
        out_specs=[row(d), row(d), row(LANES)],
        out_shape=[jax.ShapeDtypeStruct((m, d), F32), jax.ShapeDtypeStruct((m, d), F32), jax.ShapeDtypeStruct((m, LANES), F32)],
        compiler_params=_params("parallel"),
        name="route",
    )(x1, att, w_o, nf, w_router, b_router)


def _expert_kernel(te_ref, nt_ref, x_ref, wg_ref, wu_ref, wd_ref, o_ref):
    @pl.when(pl.program_id(0) < nt_ref[0])
    def _():
        x = x_ref[...].astype(BF16)
        hid = _silu(_dot(x, wg_ref[0].astype(BF16))) * _dot(x, wu_ref[0].astype(BF16))
        o_ref[...] = _dot(hid.astype(BF16), wd_ref[0].astype(BF16)).astype(o_ref.dtype)

    @pl.when(pl.program_id(0) >= nt_ref[0])
    def _():
        o_ref[...] = jnp.zeros_like(o_ref)


def _experts(x_sorted, tile_expert, n_tiles, w_gate, w_up, w_down, *, tm):
    p, d = x_sorted.shape
    ne, _, ff = w_gate.shape
    grid_spec = pltpu.PrefetchScalarGridSpec(
        num_scalar_prefetch=2,
        grid=(p // tm,),
        in_specs=[pl.BlockSpec((tm, d), lambda i, te, nt: (i, 0)),
                  pl.BlockSpec((1, d, ff), lambda i, te, nt: (te[i], 0, 0)),
                  pl.BlockSpec((1, d, ff), lambda i, te, nt: (te[i], 0, 0)),
                  pl.BlockSpec((1, ff, d), lambda i, te, nt: (te[i], 0, 0))],
        out_specs=pl.BlockSpec((tm, d), lambda i, te, nt: (i, 0)),
    )
    return pl.pallas_call(
        _expert_kernel,
        grid_spec=grid_spec,
        out_shape=jax.ShapeDtypeStruct((p, d), F32),
        compiler_params=_params("arbitrary"),
        name="experts",
    )(tile_expert, n_tiles, x_sorted, w_gate, w_up, w_down)


def _experts_dense_kernel(h_ref, rt_ref, wg_ref, wu_ref, wd_ref, o_ref):
    e = pl.program_id(0)

    @pl.when(e == 0)
    def _():
        o_ref[...] = jnp.zeros_like(o_ref)

    rt = rt_ref[...]
    ef = e.astype(F32)
    cw = jnp.where(rt[:, 0:1] == ef, rt[:, 2:3], 0.0) + jnp.where(rt[:, 1:2] == ef, rt[:, 3:4], 0.0)
    x = h_ref[...].astype(BF16)
    hid = _silu(_dot(x, wg_ref[0].astype(BF16))) * _dot(x, wu_ref[0].astype(BF16))
    o_ref[...] = o_ref[...] + cw * _dot(hid.astype(BF16), wd_ref[0].astype(BF16))


def _experts_dense(h, rt, w_gate, w_up, w_down):
    m, d = h.shape
    ne, _, ff = w_gate.shape
    return pl.pallas_call(
        _experts_dense_kernel,
        grid=(ne,),
        in_specs=[pl.BlockSpec((m, d), lambda e: (0, 0)), pl.BlockSpec((m, LANES), lambda e: (0, 0)),
                  pl.BlockSpec((1, d, ff), lambda e: (e, 0, 0)), pl.BlockSpec((1, d, ff), lambda e: (e, 0, 0)),
                  pl.BlockSpec((1, ff, d), lambda e: (e, 0, 0))],
        out_specs=pl.BlockSpec((m, d), lambda e: (0, 0)),
        out_shape=jax.ShapeDtypeStruct((m, d), F32),
        compiler_params=_params("arbitrary"),
        name="experts_dense",
    )(h, rt, w_gate, w_up, w_down)


def _final_kernel(x2_ref, ya_ref, yb_ref, rt_ref, g_ref, o_ref):
    rt = rt_ref[...]
    moe = rt[:, 2:3] * ya_ref[...] + rt[:, 3:4] * yb_ref[...]
    o_ref[...] = _rms(x2_ref[...] + moe, g_ref[...])


def _final(x2, y2, rt, norm_final, *, tm):
    m, d = x2.shape
    row = lambda w: pl.BlockSpec((tm, w), lambda i: (i, 0))
    slot = lambda k: pl.BlockSpec((None, tm, d), lambda i: (k, i, 0))
    return pl.pallas_call(
        _final_kernel,
        grid=(m // tm,),
        in_specs=[row(d), slot(0), slot(1), row(LANES), pl.BlockSpec((1, d), lambda i: (0, 0))],
        out_specs=row(d),
        out_shape=jax.ShapeDtypeStruct((m, d), F32),
        compiler_params=_params("parallel"),
        name="final",
    )(x2, y2, y2, rt, norm_final.reshape(1, d))


def _final_dense_kernel(x2_ref, moe_ref, g_ref, o_ref):
    o_ref[...] = _rms(x2_ref[...] + moe_ref[...], g_ref[...])


def _final_dense(x2, moe, norm_final):
    m, d = x2.shape
    return pl.pallas_call(
        _final_dense_kernel,
        out_shape=jax.ShapeDtypeStruct((m, d), F32),
        name="final_dense",
    )(x2, moe, norm_final.reshape(1, d))


def _dispatch_plan(eid, tm):
    n = eid.shape[0]
    p = (n + N_EXPERTS * (tm - 1) + tm - 1) // tm * tm
    onehot = (eid[:, None] == jnp.arange(N_EXPERTS, dtype=jnp.int32)[None, :]).astype(jnp.int32)
    csum = jnp.cumsum(onehot, axis=0)
    counts = csum[-1]
    rank = jnp.sum(csum * onehot, axis=1) - 1
    padded = (counts + tm - 1) // tm * tm
    pend = jnp.cumsum(padded)
    pstart = pend - padded
    pos = jnp.sum(pstart[None, :] * onehot, axis=1) + rank
    src = jnp.full((p,), -1, jnp.int32).at[pos].set(jnp.arange(n, dtype=jnp.int32), unique_indices=True)
    n_tiles = (pend[-1] // tm).astype(jnp.int32)
    tile_first = jnp.arange(p // tm, dtype=jnp.int32) * tm
    tile_expert = jnp.sum((tile_first[:, None] >= pend[None, :]).astype(jnp.int32), axis=1)
    last_used = jnp.sum((jnp.maximum(pend[-1] - tm, 0) >= pend).astype(jnp.int32))
    tile_expert = jnp.where(tile_first < pend[-1], tile_expert, last_used).astype(jnp.int32)
    return src, pos, tile_expert, n_tiles.reshape(1)


def _bf(w):
    return w.astype(BF16)


def kernel(x_prompt, x_sample, mem_prompt, cache_fox_k, cache_fox_v, cache_fox_logf, page_table, cache_mem_k, cache_mem_v, state_conv, state_ssm, norm_mix, w_in, conv_w, conv_b, dt_bias, a_log, d_skip, ssd_norm, w_ssd_out, b_fox_f, w_fox_out, b_gate, w_mix_out, norm_mem, norm_mem_kv, w_mem_q, w_mem_k, w_mem_v, w_mem_o, norm_ffn, w_router_group, b_router_group, w_router_expert, b_router_expert, w_exp_gate, w_exp_up, w_exp_down, norm_final):
    depth = w_in.shape[0]
    assert depth == 1, "single-layer step"
    bp, sp, d = x_prompt.shape
    bs, ss, _ = x_sample.shape
    tp, ts = bp * sp, bs * ss
    d_inner = ssd_norm.shape[-1]
    cdim = conv_w.shape[-1]
    nh_ssd = dt_bias.shape[-1]
    nh, dh = FOX_N_HEADS, FOX_HEAD_DIM
    fw = nh * dh
    mem_len = mem_prompt.shape[1]
    l = 0

    o_z, o_x, o_dt, o_q, o_k, o_v, o_f, o_g = np.cumsum([0, d_inner, cdim, nh_ssd, fw, fw, fw, nh]).tolist()
    wi = w_in[l]
    tn = 1024
    seg = lambda a, b: _tile_cols(_bf(wi[:, a:b]), tn)
    w_z, w_x, w_qf, w_kf, w_vf, w_g = seg(o_z, o_x), seg(o_x, o_dt), seg(o_q, o_k), seg(o_k, o_v), seg(o_v, o_f), seg(o_g, o_g + 2 * d)
    n_small = nh_ssd + nh
    w_small = _bf(jnp.pad(jnp.concatenate([wi[:, o_dt:o_q], wi[:, o_f:o_g]], axis=1), ((0, 0), (0, LANES - n_small))))[None]
    f_off = nh_ssd
    w_so, w_fo, w_mx, w_q, w_o = _bf(w_ssd_out[l]), _bf(w_fox_out[l]), _bf(w_mix_out[l]), _bf(w_mem_q[l]), _bf(w_mem_o[l])
    n_r = N_EXPERT_GROUPS + N_EXPERTS
    w_router = jnp.pad(jnp.concatenate([w_router_group[l], w_router_expert[l]], axis=1), ((0, 0), (0, LANES - n_r)))
    wr_hi = _bf(w_router)
    wr_lo = _bf(w_router - wr_hi.astype(F32))
    w_router = jnp.concatenate([wr_hi, wr_lo, wr_hi], axis=0)
    b_router = jnp.pad(jnp.concatenate([b_router_group[l], b_router_expert[l]]), (0, LANES - n_r)).reshape(1, LANES)
    mem_scale = (d // MEM_N_HEADS) ** -0.5

    w_qkv = jnp.concatenate([w_qf, w_kf, w_vf], axis=0)

    def in_proj(x2d, tm, qkv_dtype, q_scale):
        h = _rmsnorm(x2d, norm_mix[l], tm)
        mm = lambda w, dt=F32, sc=1.0: _matmul(h, w, dt, tm=tm, scale=sc)
        return (mm(w_z), mm(w_x), mm(w_qkv, qkv_dtype, q_scale), mm(w_g), mm(w_small)), h

    def post(x2d, y_ssd, y_fox, gates, mem_k, mem_v, nb, tm, tl):
        per = x2d.shape[0] // nb
        q_dtype = BF16 if per % 16 == 0 else F32
        x1, qm = _mix(x2d, y_ssd, y_fox, gates, b_gate[l], w_so, w_fo, w_mx, norm_mem[l], w_q, tm=tm, qscale=mem_scale, q_dtype=q_dtype)
        qm = qm.reshape(nb, per, d)
        if per % 8:
            qm = jnp.pad(qm, ((0, 0), (0, 8 - per % 8), (0, 0)))
        att = _mem_attn(qm, mem_k, mem_v, tl=tl)[:, :per].reshape(-1, d).astype(BF16)
        return _route(x1, att, w_o, norm_ffn[l], w_router, b_router, tm=tm)

    xp = x_prompt.reshape(tp, d)
    (z, xbc, qkv, gates, small), h_in = in_proj(xp, 1024, BF16, dh ** -0.5 * LOG2E)
    qkv = qkv.reshape(bp, sp, 3 * fw)
    k_out = _matmul_t(_bf(wi[:, o_k:o_v]).T[None], h_in, seqs=bp, tm=1024).reshape(bp, nh, dh, sp).transpose(0, 3, 1, 2)
    v_out = _matmul_t(_bf(wi[:, o_v:o_f]).T[None], h_in, seqs=bp, tm=1024).reshape(bp, nh, dh, sp).transpose(0, 3, 1, 2)
    conv0 = jnp.zeros((bp, SSD_CONV - 1, cdim), F32)
    ssm0 = jnp.zeros((bp, nh_ssd, SSD_HEAD_DIM, SSD_D_STATE), F32)
    y_ssd, pconv, pssm = _ssd(xbc.reshape(bp, sp, cdim), z.reshape(bp, sp, d_inner), small.reshape(bp, sp, LANES), conv0, ssm0,
                              conv_w[l], conv_b[l], dt_bias[l], a_log[l], d_skip[l], ssd_norm[l], T=128, CPS=4, valid=None)
    plogf, q_aug, k_aug = _fox_gate(small.reshape(bp, sp, LANES), qkv, b_fox_f[l], tr=512, f_off=f_off)
    y_fox = _fox_prompt(q_aug, k_aug, qkv, tq=1024, hps=4, v_col=2 * fw)
    mem_h = _rmsnorm(mem_prompt.reshape(bp * mem_len, d), norm_mem_kv[l], 256)
    mk = _matmul(mem_h, _tile_cols(_bf(w_mem_k[l]), 512), F32, tm=256)
    mv = _matmul(mem_h, _tile_cols(_bf(w_mem_v[l]), 512), F32, tm=256)
    x2p, hp, rtp = post(xp, y_ssd.reshape(tp, d_inner), y_fox.reshape(tp, fw), gates, mk.reshape(bp, mem_len, d), mv.reshape(bp, mem_len, d), bp, 256, 512)

    tm_e = 256
    eid = rtp[:, :2].astype(jnp.int32).reshape(-1)
    src, pos, tile_expert, n_tiles = _dispatch_plan(eid, tm_e)
    take = lambda a, idx: a.at[idx].get(mode="promise_in_bounds")
    xs = x_sample.reshape(ts, d)
    (zs, xbcs, qkv_s, gates_s, small_s), _ = in_proj(xs, ts, F32, dh ** -0.5)
    qs, ks, vs = qkv_s[:, :fw], qkv_s[:, fw:2 * fw], qkv_s[:, 2 * fw:]
    pad8 = lambda a: jnp.pad(a.reshape(bs, ss, -1), ((0, 0), (0, 8 - ss), (0, 0)))
    ys_ssd, sconv, sssm = _ssd(pad8(xbcs), pad8(zs), pad8(small_s), state_conv[l], state_ssm[l],
                               conv_w[l], conv_b[l], dt_bias[l], a_log[l], d_skip[l], ssd_norm[l], T=8, CPS=1, valid=ss)
    pool, page = cache_fox_k.shape[1], cache_fox_k.shape[2]
    kt = jnp.transpose(cache_fox_k[l], (0, 2, 3, 1)).reshape(pool, fw, page)
    vt = jnp.transpose(cache_fox_v[l], (0, 2, 3, 1)).reshape(pool, fw, page)
    gates_past = _gate_scan(jnp.transpose(cache_fox_logf[l], (0, 2, 1)).reshape(pool * nh, page), rows=2048)
    fs_in = (pad8(qs), pad8(ks), pad8(vs), pad8(small_s), gates_past, ys_ssd)
    src, fs_in = lax.optimization_barrier((src, fs_in))
    x_sorted = take(hp, jnp.maximum(src, 0) // 2)
    ys_fox, slogf = _fox_sample(*fs_in[:4], b_fox_f[l], kt, vt, fs_in[4], page_table, nq=ss, PG=16, f_off=f_off)
    ys_ssd = fs_in[5]
    x2s, hs, rts = post(xs, ys_ssd[:, :ss].reshape(ts, d_inner).astype(BF16), ys_fox[:, :ss].reshape(ts, fw).astype(BF16), gates_s,
                        cache_mem_k[l], cache_mem_v[l], bs, ts, 8)

    moe_s = _experts_dense(hs, rts, w_exp_gate[l], w_exp_up[l], w_exp_down[l])
    y_sample = _final_dense(x2s, moe_s, norm_final).reshape(bs, ss, d)

    x_sorted, y_sample = lax.optimization_barrier((x_sorted, y_sample))
    y_sorted = _experts(x_sorted, tile_expert, n_tiles, w_exp_gate[l], w_exp_up[l], w_exp_down[l], tm=tm_e)
    y_prompt = _final(x2p, take(y_sorted, pos.reshape(tp, 2).T), rtp, norm_final, tm=512).reshape(bp, sp, d)

    return (y_prompt, y_sample,
            k_out[None], v_out[None], plogf.reshape(1, bp, sp, nh),
            mk.reshape(1, bp, mem_len, MEM_N_HEADS, d // MEM_N_HEADS), mv.reshape(1, bp, mem_len, MEM_N_HEADS, d // MEM_N_HEADS),
            pconv[None], pssm[None],
            ks.reshape(1, bs, ss, nh, dh), vs.reshape(1, bs, ss, nh, dh), slogf[:, :ss][None],
            sconv[None], sssm[None])
```
